```python
import math
import jax, jax.numpy as jnp
from jax import lax
import numpy as np

D_MODEL = 2048
BATCH = 2
SEQ = 8192
DEPTH = 4

SSD_WIDTH = D_MODEL // 2
NSA_WIDTH = D_MODEL // 4
CONV_CH = D_MODEL // 4
D_MIX = SSD_WIDTH + NSA_WIDTH + CONV_CH
SSD_HEAD_DIM = 64
SSD_HEADS = SSD_WIDTH // SSD_HEAD_DIM
SSD_GROUPS = 2
SSD_STATE = 128
SSD_CONV = 4
SSD_CHUNK = 128
SSD_XBC = SSD_WIDTH + 2 * SSD_GROUPS * SSD_STATE
NSA_HEAD_DIM = 64
NSA_HEADS = NSA_WIDTH // NSA_HEAD_DIM
NSA_KV_GROUPS = 2
NSA_KV = NSA_KV_GROUPS * NSA_HEAD_DIM
CMP_BLOCK = 32
CMP_STRIDE = 16
CMP_HIDDEN = 256
SLC_BLOCK = 64
SLC_TOPN = 16
WINDOW = 512
Q_BLOCK = 128
FORCE_BONUS = 1000.0
CONV_KERNEL = 31
D_FF = 5632
HALF = 0.5
EPS = 1e-6
NEG = -1e30

IN_SPLITS = (SSD_WIDTH, SSD_XBC, SSD_HEADS,
             NSA_WIDTH, NSA_KV, NSA_KV, NSA_KV, NSA_KV, NSA_KV, NSA_KV, 3 * NSA_HEADS,
             2 * CONV_CH)
D_IN = sum(IN_SPLITS)

kernel_name = "hybrid_ssd_nsa_conformer_macaron"

F32 = jnp.float32


def rms_norm(u, w):
    uf = u.astype(F32)
    y = uf * lax.rsqrt(jnp.mean(uf * uf, axis=-1, keepdims=True) + EPS)
    return (y * w.astype(F32)).astype(u.dtype)


def swiglu(u, w_gu, w_down):
    g, v = jnp.split(u @ w_gu, 2, axis=-1)
    return (jax.nn.silu(g) * v) @ w_down


def causal_depthwise_conv(u, w, b):
    k, c = w.shape
    y = lax.conv_general_dilated(u, w[:, None, :].astype(u.dtype), window_strides=(1,),
                                 padding=[(k - 1, 0)], dimension_numbers=('NWC', 'WIO', 'NWC'),
                                 feature_group_count=c)
    return y + b.astype(y.dtype)


def masked_softmax(s, valid):
    s = jnp.where(valid, s.astype(F32), NEG)
    return jax.nn.softmax(s, axis=-1) * valid


def ssd_scan(x, dt, a, bm, cm):
    bsz, s, h, p = x.shape
    g, n = bm.shape[2], bm.shape[3]
    hg = h // g
    l = SSD_CHUNK
    nc = s // l
    xc = x.astype(F32).reshape(bsz, nc, l, g, hg, p)
    dtc = dt.reshape(bsz, nc, l, g, hg)
    bc = bm.astype(F32).reshape(bsz, nc, l, g, n)
    cc = cm.astype(F32).reshape(bsz, nc, l, g, n)
    cs = jnp.cumsum(dtc * a.reshape(g, hg), axis=2)
    tril = jnp.tril(jnp.ones((l, l), bool))[:, :, None, None]
    seg = cs[:, :, :, None] - cs[:, :, None, :]
    decay = jnp.exp(jnp.where(tril, seg, -jnp.inf))
    cb = jnp.einsum('bclgn,bcsgn->bclsg', cc, bc)
    w = cb[..., None] * decay * dtc[:, :, None]
    y_diag = jnp.einsum('bclsgh,bcsghp->bclghp', w, xc)
    decay_end = jnp.exp(cs[:, :, -1:] - cs) * dtc
    states = jnp.einsum('bcsgn,bcsghp->bcghpn', bc, xc * decay_end[..., None])
    chunk_decay = jnp.exp(cs[:, :, -1])

    def step(hs, inp):
        st, dec = inp
        return hs * dec[..., None, None] + st, hs

    h0 = jnp.zeros((bsz, g, hg, p, n), F32)
    _, h_prev = lax.scan(step, h0, (jnp.moveaxis(states, 1, 0), jnp.moveaxis(chunk_decay, 1, 0)))
    h_prev = jnp.moveaxis(h_prev, 0, 1)
    y_off = jnp.einsum('bclgn,bcghpn->bclghp', cc, h_prev) * jnp.exp(cs)[..., None]
    return (y_diag + y_off).reshape(bsz, s, h, p)


def ssd_mixer(z, xbc, dt_raw, conv_w, conv_b, dt_bias, a_log, d_skip, norm_w):
    bsz, s, _ = z.shape
    xbc = jax.nn.silu(causal_depthwise_conv(xbc, conv_w, conv_b))
    xs, bm, cm = jnp.split(xbc, [SSD_WIDTH, SSD_WIDTH + SSD_GROUPS * SSD_STATE], axis=-1)
    xs = xs.reshape(bsz, s, SSD_HEADS, SSD_HEAD_DIM)
    bm = bm.reshape(bsz, s, SSD_GROUPS, SSD_STATE)
    cm = cm.reshape(bsz, s, SSD_GROUPS, SSD_STATE)
    dt = jax.nn.softplus(dt_raw.astype(F32) + dt_bias.astype(F32))
    a = -jnp.exp(a_log.astype(F32))
    y = ssd_scan(xs, dt, a, bm, cm) + d_skip.astype(F32)[:, None] * xs.astype(F32)
    gsz = SSD_WIDTH // SSD_GROUPS
    y = y.reshape(bsz, s, SSD_GROUPS, gsz) * jax.nn.silu(z.astype(F32)).reshape(bsz, s, SSD_GROUPS, gsz)
    y = y * lax.rsqrt(jnp.mean(y * y, axis=-1, keepdims=True) + EPS)
    return (y.reshape(bsz, s, SSD_WIDTH) * norm_w.astype(F32)).astype(z.dtype)


def compress_blocks(u, blk_idx, pe, w1, b1, w2):
    bsz = u.shape[0]
    n_cmp = blk_idx.shape[0]
    blocks = u[:, blk_idx] + pe[:, None, :].astype(u.dtype)
    blocks = blocks.transpose(0, 3, 1, 2, 4).reshape(bsz, NSA_KV_GROUPS, n_cmp, CMP_BLOCK * NSA_HEAD_DIM)
    return jax.nn.silu(blocks @ w1 + b1) @ w2


def cmp_to_slc_weights(n_cmp, nb):
    c0 = np.arange(n_cmp)[:, None] * CMP_STRIDE
    s0 = np.arange(nb)[None, :] * SLC_BLOCK
    ov = np.minimum(c0 + CMP_BLOCK, s0 + SLC_BLOCK) - np.maximum(c0, s0)
    return jnp.asarray(np.maximum(ov, 0) / CMP_STRIDE, dtype=F32)


def nsa_mixer(q, kc_raw, vc_raw, ks, vs, kw, vw, gate_logits,
              k_pe, k_w1, k_b1, k_w2, v_pe, v_w1, v_b1, v_w2):
    bsz, s, _ = q.shape
    g, hg, dh = NSA_KV_GROUPS, NSA_HEADS // NSA_KV_GROUPS, NSA_HEAD_DIM
    n_cmp = (s - CMP_BLOCK) // CMP_STRIDE + 1
    nb = s // SLC_BLOCK
    n_sel = min(SLC_TOPN, nb)
    nqb = s // Q_BLOCK
    scale = dh ** -0.5
    kv = lambda a: a.reshape(bsz, s, g, dh)
    qh = q.reshape(bsz, s, g, hg, dh).transpose(0, 2, 3, 1, 4)
    gates = jax.nn.sigmoid(gate_logits.astype(F32)).reshape(bsz, s, g, hg, 3).transpose(0, 2, 3, 1, 4)
    blk_idx = np.arange(n_cmp)[:, None] * CMP_STRIDE + np.arange(CMP_BLOCK)[None, :]
    kc = compress_blocks(kv(kc_raw), blk_idx, k_pe, k_w1, k_b1, k_w2)
    vc = compress_blocks(kv(vc_raw), blk_idx, v_pe, v_w1, v_b1, v_w2)
    cmp_end = jnp.asarray(np.arange(n_cmp) * CMP_STRIDE + CMP_BLOCK - 1, jnp.int32)
    overlap = cmp_to_slc_weights(n_cmp, nb)
    ksb = kv(ks).reshape(bsz, nb, SLC_BLOCK, g, dh).transpose(0, 3, 1, 2, 4)
    vsb = kv(vs).reshape(bsz, nb, SLC_BLOCK, g, dh).transpose(0, 3, 1, 2, 4)
    pad = ((0, 0), (0, 0), (WINDOW, 0), (0, 0))
    kwp = jnp.pad(kv(kw).transpose(0, 2, 1, 3), pad)
    vwp = jnp.pad(kv(vw).transpose(0, 2, 1, 3), pad)
    blk_start = jnp.arange(nb) * SLC_BLOCK
    blk_id = jnp.arange(nb)
    gather = jax.vmap(jax.vmap(lambda tbl, i: tbl[i]))

    def one_block(qi):
        q0 = qi * Q_BLOCK
        t = q0 + jnp.arange(Q_BLOCK)
        qb = lax.dynamic_slice_in_dim(qh, q0, Q_BLOCK, axis=3)
        gb = lax.dynamic_slice_in_dim(gates, q0, Q_BLOCK, axis=3)
        valid_c = cmp_end[None, :] <= t[:, None]
        p_c = masked_softmax(jnp.einsum('bghqd,bgnd->bghqn', qb, kc) * scale, valid_c)
        o_c = jnp.einsum('bghqn,bgnd->bghqd', p_c, vc)
        imp = jnp.einsum('bghqn,nj->bgqj', p_c, overlap)
        cur = t // SLC_BLOCK
        forced = (blk_id[None, :] == 0) | (blk_id[None, :] == cur[:, None]) | (blk_id[None, :] == cur[:, None] - 1)
        imp = jnp.where(blk_start[None, :] <= t[:, None], imp + FORCE_BONUS * forced, NEG)
        _, sel = lax.top_k(imp, n_sel)
        k_sel = gather(ksb, sel).reshape(bsz, g, Q_BLOCK, n_sel * SLC_BLOCK, dh)
        v_sel = gather(vsb, sel).reshape(bsz, g, Q_BLOCK, n_sel * SLC_BLOCK, dh)
        pos_s = (sel[..., None] * SLC_BLOCK + jnp.arange(SLC_BLOCK)).reshape(bsz, g, Q_BLOCK, n_sel * SLC_BLOCK)
        valid_s = (pos_s <= t[:, None])[:, :, None]
        p_s = masked_softmax(jnp.einsum('bghqd,bgqkd->bghqk', qb, k_sel) * scale, valid_s)
        o_s = jnp.einsum('bghqk,bgqkd->bghqd', p_s, v_sel)
        k_w = lax.dynamic_slice_in_dim(kwp, q0, WINDOW + Q_BLOCK, axis=2)
        v_w = lax.dynamic_slice_in_dim(vwp, q0, WINDOW + Q_BLOCK, axis=2)
        pos_w = q0 - WINDOW + jnp.arange(WINDOW + Q_BLOCK)
        dist = t[:, None] - pos_w[None, :]
        valid_w = (dist >= 0) & (dist < WINDOW) & (pos_w[None, :] >= 0)
        p_w = masked_softmax(jnp.einsum('bghqd,bgkd->bghqk', qb, k_w) * scale, valid_w)
        o_w = jnp.einsum('bghqk,bgkd->bghqd', p_w, v_w)
        return gb[..., 0:1] * o_c + gb[..., 1:2] * o_s + gb[..., 2:3] * o_w

    o = lax.map(one_block, jnp.arange(nqb))
    return o.transpose(1, 0, 4, 2, 3, 5).reshape(bsz, s, NSA_WIDTH)


def conformer_conv(glu_in, dw_w, dw_b, ln_w, ln_b):
    a, gt = jnp.split(glu_in, 2, axis=-1)
    v = causal_depthwise_conv(a * jax.nn.sigmoid(gt), dw_w, dw_b)
    vf = v.astype(F32)
    mu = jnp.mean(vf, axis=-1, keepdims=True)
    var = jnp.mean(jnp.square(vf - mu), axis=-1, keepdims=True)
    vf = (vf - mu) * lax.rsqrt(var + EPS) * ln_w.astype(F32) + ln_b.astype(F32)
    return jax.nn.silu(vf).astype(glu_in.dtype)


def hybrid_mixer(u, w_in, ssd_conv_w, ssd_conv_b, ssd_dt_bias, ssd_a_log, ssd_d, ssd_norm_w,
                 nsa_k_pe, nsa_k_w1, nsa_k_b1, nsa_k_w2, nsa_v_pe, nsa_v_w1, nsa_v_b1, nsa_v_w2,
                 cnv_dw_w, cnv_dw_b, cnv_ln_w, cnv_ln_b, w_out):
    proj = u @ w_in
    (z, xbc, dt_raw, q, kc, vc, ks, vs, kw, vw, gl, glu) = jnp.split(
        proj, np.cumsum(IN_SPLITS)[:-1].tolist(), axis=-1)
    y_ssd = ssd_mixer(z, xbc, dt_raw, ssd_conv_w, ssd_conv_b, ssd_dt_bias, ssd_a_log, ssd_d, ssd_norm_w)
    y_nsa = nsa_mixer(q, kc, vc, ks, vs, kw, vw, gl, nsa_k_pe, nsa_k_w1, nsa_k_b1, nsa_k_w2,
                      nsa_v_pe, nsa_v_w1, nsa_v_b1, nsa_v_w2)
    y_cnv = conformer_conv(glu, cnv_dw_w, cnv_dw_b, cnv_ln_w, cnv_ln_b)
    y = jnp.concatenate([y_ssd.astype(u.dtype), y_nsa.astype(u.dtype), y_cnv.astype(u.dtype)], axis=-1)
    return y @ w_out


def setup_inputs(seed: int = 0) -> dict:
    key = jax.random.key(seed)
    keys = iter(jax.random.split(key, 40))
    L, D = DEPTH, D_MODEL

    def nrm(shape, scale):
        return jax.random.normal(next(keys), shape, F32) * scale

    def gain(shape):
        return 1.0 + nrm(shape, 0.05)

    x = nrm((BATCH, SEQ, D), 1.0)
    ffn1_pre_norm = gain((L, D))
    ffn1_w_gu = nrm((L, D, 2 * D_FF), D ** -0.5)
    ffn1_w_down = nrm((L, D_FF, D), D_FF ** -0.5)
    ffn1_post_norm = gain((L, D))
    mix_pre_norm = gain((L, D))
    w_in = nrm((L, D, D_IN), D ** -0.5)
    ssd_conv_w = nrm((L, SSD_CONV, SSD_XBC), SSD_CONV ** -0.5)
    ssd_conv_b = nrm((L, SSD_XBC), 0.02)
    dt0 = jnp.exp(jax.random.uniform(next(keys), (L, SSD_HEADS), F32, math.log(1e-3), math.log(1e-1)))
    ssd_dt_bias = dt0 + jnp.log(-jnp.expm1(-dt0))
    ssd_a_log = jnp.log(jax.random.uniform(next(keys), (L, SSD_HEADS), F32, 1.0, 16.0))
    ssd_d = gain((L, SSD_HEADS))
    ssd_norm_w = gain((L, SSD_WIDTH))
    fan_c = CMP_BLOCK * NSA_HEAD_DIM
    nsa_k_pe = nrm((L, CMP_BLOCK, NSA_HEAD_DIM), 0.1)
    nsa_k_w1 = nrm((L, fan_c, CMP_HIDDEN), fan_c ** -0.5)
    nsa_k_b1 = nrm((L, CMP_HIDDEN), 0.02)
    nsa_k_w2 = nrm((L, CMP_HIDDEN, NSA_HEAD_DIM), CMP_HIDDEN ** -0.5)
    nsa_v_pe = nrm((L, CMP_BLOCK, NSA_HEAD_DIM), 0.1)
    nsa_v_w1 = nrm((L, fan_c, CMP_HIDDEN), fan_c ** -0.5)
    nsa_v_b1 = nrm((L, CMP_HIDDEN), 0.02)
    nsa_v_w2 = nrm((L, CMP_HIDDEN, NSA_HEAD_DIM), CMP_HIDDEN ** -0.5)
    cnv_dw_w = nrm((L, CONV_KERNEL, CONV_CH), CONV_KERNEL ** -0.5)
    cnv_dw_b = nrm((L, CONV_CH), 0.02)
    cnv_ln_w = gain((L, CONV_CH))
    cnv_ln_b = nrm((L, CONV_CH), 0.02)
    w_out = nrm((L, D_MIX, D), D_MIX ** -0.5)
    mix_post_norm = gain((L, D))
    ffn2_pre_norm = gain((L, D))
    ffn2_w_gu = nrm((L, D, 2 * D_FF), D ** -0.5)
    ffn2_w_down = nrm((L, D_FF, D), D_FF ** -0.5)
    ffn2_post_norm = gain((L, D))
    return {"x": x, "ffn1_pre_norm": ffn1_pre_norm, "ffn1_w_gu": ffn1_w_gu, "ffn1_w_down": ffn1_w_down,
            "ffn1_post_norm": ffn1_post_norm, "mix_pre_norm": mix_pre_norm, "w_in": w_in,
            "ssd_conv_w": ssd_conv_w, "ssd_conv_b": ssd_conv_b, "ssd_dt_bias": ssd_dt_bias,
            "ssd_a_log": ssd_a_log, "ssd_d": ssd_d, "ssd_norm_w": ssd_norm_w,
            "nsa_k_pe": nsa_k_pe, "nsa_k_w1": nsa_k_w1, "nsa_k_b1": nsa_k_b1, "nsa_k_w2": nsa_k_w2,
            "nsa_v_pe": nsa_v_pe, "nsa_v_w1": nsa_v_w1, "nsa_v_b1": nsa_v_b1, "nsa_v_w2": nsa_v_w2,
            "cnv_dw_w": cnv_dw_w, "cnv_dw_b": cnv_dw_b, "cnv_ln_w": cnv_ln_w, "cnv_ln_b": cnv_ln_b,
            "w_out": w_out, "mix_post_norm": mix_post_norm, "ffn2_pre_norm": ffn2_pre_norm,
            "ffn2_w_gu": ffn2_w_gu, "ffn2_w_down": ffn2_w_down, "ffn2_post_norm": ffn2_post_norm}


def reference(x, ffn1_pre_norm, ffn1_w_gu, ffn1_w_down, ffn1_post_norm, mix_pre_norm, w_in,
              ssd_conv_w, ssd_conv_b, ssd_dt_bias, ssd_a_log, ssd_d, ssd_norm_w,
              nsa_k_pe, nsa_k_w1, nsa_k_b1, nsa_k_w2, nsa_v_pe, nsa_v_w1, nsa_v_b1, nsa_v_w2,
              cnv_dw_w, cnv_dw_b, cnv_ln_w, cnv_ln_b, w_out, mix_post_norm,
              ffn2_pre_norm, ffn2_w_gu, ffn2_w_down, ffn2_post_norm):
    h = x
    for l in range(DEPTH):
        u = rms_norm(h, ffn1_pre_norm[l])
        h = h + HALF * rms_norm(swiglu(u, ffn1_w_gu[l], ffn1_w_down[l]), ffn1_post_norm[l])
        u = rms_norm(h, mix_pre_norm[l])
        m = hybrid_mixer(u, w_in[l], ssd_conv_w[l], ssd_conv_b[l], ssd_dt_bias[l], ssd_a_log[l],
                         ssd_d[l], ssd_norm_w[l], nsa_k_pe[l], nsa_k_w1[l], nsa_k_b1[l], nsa_k_w2[l],
                         nsa_v_pe[l], nsa_v_w1[l], nsa_v_b1[l], nsa_v_w2[l],
                         cnv_dw_w[l], cnv_dw_b[l], cnv_ln_w[l], cnv_ln_b[l], w_out[l])
        h = h + rms_norm(m, mix_post_norm[l])
        u = rms_norm(h, ffn2_pre_norm[l])
        h = h + HALF * rms_norm(swiglu(u, ffn2_w_gu[l], ffn2_w_down[l]), ffn2_post_norm[l])
    return h
```

```python
import functools

import numpy as np
import jax
import jax.numpy as jnp
from jax import lax
from jax.experimental import pallas as pl
from jax.experimental.pallas import tpu as pltpu

F32 = jnp.float32
BF16 = jnp.bfloat16

SSD_HEAD_DIM = 64
SSD_GROUPS = 2
SSD_STATE = 128
SSD_CONV = 4
SSD_CHUNK = 128
NSA_HEAD_DIM = 64
NSA_KV_GROUPS = 2
CMP_BLOCK = 32
CMP_STRIDE = 16
SLC_BLOCK = 64
SLC_TOPN = 16
WINDOW = 512
Q_BLOCK = 128
FORCE_BONUS = 1000.0
CONV_KERNEL = 31
HALF = 0.5
EPS = 1e-6
NEG = -1e30

LANES = 128
SUBLANES = 8
VMEM_LIMIT = 56 * 1024 * 1024
TOKEN_TILE = 512
FF_TILE = 512
KV_TILE = 512
CONV_TILE = 256
CONV_HALO = 32


def _cparams(sem):
    return pltpu.CompilerParams(dimension_semantics=sem, vmem_limit_bytes=VMEM_LIMIT)


def _rms(x, w):
    return (x * lax.rsqrt(jnp.mean(x * x, axis=-1, keepdims=True) + EPS)) * w


def _silu(x):
    return x * jax.nn.sigmoid(x)


def _dot(a, b):
    return jnp.dot(a, b, preferred_element_type=F32)


def _dot_nt(a, b):
    return lax.dot_general(a, b, (((1,), (1,)), ((), ())), preferred_element_type=F32)


def _split3(x):
    hi = x.astype(BF16)
    r1 = x - hi.astype(F32)
    mid = r1.astype(BF16)
    lo = (r1 - mid.astype(F32)).astype(BF16)
    return hi, mid, lo


def _dot_exact_lhs(a, x):
    hi, mid, lo = _split3(x)
    return _dot(a, hi) + _dot(a, mid) + _dot(a, lo)


def _ffn_kernel(h_ref, pre_ref, wg_ref, wu_ref, wd_ref, post_ref, o_ref, u_scr, acc_scr):
    j = pl.program_id(1)

    @pl.when(j == 0)
    def _():
        u_scr[...] = _rms(h_ref[...], pre_ref[...]).astype(BF16)
        acc_scr[...] = jnp.zeros_like(acc_scr)

    u = u_scr[...]
    g = _dot(u, wg_ref[...])
    v = _dot(u, wu_ref[...])
    acc_scr[...] += _dot((_silu(g) * v).astype(BF16), wd_ref[...])

    @pl.when(j == pl.num_programs(1) - 1)
    def _():
        o_ref[...] = h_ref[...] + HALF * _rms(acc_scr[...], post_ref[...])


def _ffn(h, pre_w, w_gu, w_down, post_w, layer):
    n, d = h.shape
    f = w_down.shape[1]
    nf = f // FF_TILE
    return pl.pallas_call(
        _ffn_kernel,
        grid=(n // TOKEN_TILE, nf),
        in_specs=[
            pl.BlockSpec((TOKEN_TILE, d), lambda i, j: (i, 0)),
            pl.BlockSpec((None, 1, d), lambda i, j: (layer, 0, 0)),
            pl.BlockSpec((None, d, FF_TILE), lambda i, j: (layer, 0, j)),
            pl.BlockSpec((None, d, FF_TILE), lambda i, j: (layer, 0, j + nf)),
            pl.BlockSpec((None, FF_TILE, d), lambda i, j: (layer, j, 0)),
            pl.BlockSpec((None, 1, d), lambda i, j: (layer, 0, 0)),
        ],
        out_specs=pl.BlockSpec((TOKEN_TILE, d), lambda i, j: (i, 0)),
        out_shape=jax.ShapeDtypeStruct((n, d), F32),
        scratch_shapes=[pltpu.VMEM((TOKEN_TILE, d), BF16), pltpu.VMEM((TOKEN_TILE, d), F32)],
        compiler_params=_cparams(("parallel", "arbitrary")),
        name="ffn",
    )(h, pre_w, w_gu, w_gu, w_down, post_w)


def _inproj_kernel(h_ref, nw_ref, w_ref, o_ref, u_scr):
    @pl.when(pl.program_id(1) == 0)
    def _():
        u_scr[...] = _rms(h_ref[...], nw_ref[...]).astype(BF16)

    o_ref[...] = _dot(u_scr[...], w_ref[...])


def _inproj(h, nw, w, layer, col_tile):
    n, d = h.shape
    cols = w.shape[2]
    return pl.pallas_call(
        _inproj_kernel,
        grid=(n // TOKEN_TILE, cols // col_tile),
        in_specs=[
            pl.BlockSpec((TOKEN_TILE, d), lambda i, j: (i, 0)),
            pl.BlockSpec((None, 1, d), lambda i, j: (layer, 0, 0)),
            pl.BlockSpec((None, d, col_tile), lambda i, j: (layer, 0, j)),
        ],
        out_specs=pl.BlockSpec((TOKEN_TILE, col_tile), lambda i, j: (i, j)),
        out_shape=jax.ShapeDtypeStruct((n, cols), F32),
        scratch_shapes=[pltpu.VMEM((TOKEN_TILE, d), BF16)],
        compiler_params=_cparams(("parallel", "arbitrary")),
        name="inproj",
    )(h, nw, w)


def _ssd_kernel(z_ref, xs_ref, bc_ref, dt_ref, cwx_ref, cbx_ref, cwb_ref, cbb_ref, dtb_ref,
                alog_ref, dsk_ref, nw_ref, e_ref, o_ref, xpad, bpad, hst, *, n_heads):
    L = SSD_CHUNK
    N = SSD_STATE
    P = SSD_HEAD_DIM
    G = SSD_GROUPS
    hg = n_heads // G
    gw = hg * P
    halo = SUBLANES

    @pl.when(pl.program_id(1) == 0)
    def _():
        xpad[0:halo, :] = jnp.zeros((halo, xpad.shape[1]), F32)
        bpad[0:halo, :] = jnp.zeros((halo, bpad.shape[1]), F32)
        hst[...] = jnp.zeros_like(hst)

    xpad[halo:halo + L, :] = xs_ref[...]
    bpad[halo:halo + L, :] = bc_ref[...]

    def conv_silu(pad, w_ref, b_ref):
        acc = b_ref[...]
        for k in range(SSD_CONV):
            off = halo - (SSD_CONV - 1) + k
            acc = acc + w_ref[k:k + 1, :] * pad[off:off + L, :]
        return _silu(acc)

    x = conv_silu(xpad, cwx_ref, cbx_ref)
    bcv = conv_silu(bpad, cwb_ref, cbb_ref)
    xpad[0:halo, :] = xpad[L:L + halo, :]
    bpad[0:halo, :] = bpad[L:L + halo, :]

    lane = lax.broadcasted_iota(jnp.int32, (1, LANES), 1)
    dtv = dt_ref[...] + dtb_ref[...]
    dt = jnp.maximum(dtv, 0.0) + jnp.log(1.0 + jnp.exp(-jnp.abs(dtv)))
    dt = jnp.where(lane < n_heads, dt, 0.0)
    a = -jnp.exp(alog_ref[...])
    da = dt * a
    row = lax.broadcasted_iota(jnp.int32, (L, L), 0)
    col = lax.broadcasted_iota(jnp.int32, (L, L), 1)
    tril = row >= col
    cs = _dot_exact_lhs(tril.astype(BF16), da)
    cs_t = cs.T
    dt_t = dt.T
    cs_last = cs[L - 1:L, :]
    e = e_ref[...]
    ecs = _dot(jnp.exp(cs).astype(BF16), e)
    dend = _dot((jnp.exp(cs_last - cs) * dt).astype(BF16), e)
    lane_p = lax.broadcasted_iota(jnp.int32, (L, 2 * P), 1)

    ys = []
    for g in range(G):
        bm = bcv[:, g * N:(g + 1) * N]
        cm = bcv[:, G * N + g * N:G * N + (g + 1) * N]
        bm16 = bm.astype(BF16)
        cm16 = cm.astype(BF16)
        cb = _dot_nt(cm16, bm16)
        xg = x[:, g * gw:(g + 1) * gw]
        yd = []
        for hp in range(hg // 2):
            ws = []
            for hh in range(2):
                h = g * hg + hp * 2 + hh
                seg = cs[:, h:h + 1] - cs_t[h:h + 1, :]
                dec = jnp.exp(jnp.where(tril, seg, -jnp.inf))
                ws.append(cb * dec * dt_t[h:h + 1, :])
            wcat = jnp.concatenate(ws, axis=1).astype(BF16)
            xp = xg[:, hp * 2 * P:(hp + 1) * 2 * P]
            xbd = jnp.concatenate([jnp.where(lane_p < P, xp, 0.0),
                                   jnp.where(lane_p >= P, xp, 0.0)], axis=0).astype(BF16)
            yd.append(_dot(wcat, xbd))
        y_diag = jnp.concatenate(yd, axis=1)
        hprev = hst[g]
        y_off = _dot(cm16, hprev.astype(BF16)) * ecs[:, g * gw:(g + 1) * gw]
        st = _dot(bm.T.astype(BF16), (xg * dend[:, g * gw:(g + 1) * gw]).astype(BF16))
        hst[g] = hprev * ecs[L - 1:L, g * gw:(g + 1) * gw] + st
        y = y_diag + y_off + dsk_ref[:, g * gw:(g + 1) * gw] * xg
        y = y * _silu(z_ref[:, g * gw:(g + 1) * gw])
        y = y * lax.rsqrt(jnp.mean(y * y, axis=-1, keepdims=True) + EPS)
        ys.append(y * nw_ref[:, g * gw:(g + 1) * gw])
    o_ref[...] = jnp.concatenate(ys, axis=1).astype(o_ref.dtype)


def _ssd(proj3, cols, cwx, cbx, cwb, cbb, dtb, alog, dsk, nw, e, layer, n_heads):
    b, s, _ = proj3.shape
    width = n_heads * SSD_HEAD_DIM
    bcw = 2 * SSD_GROUPS * SSD_STATE
    L = SSD_CHUNK
    gw = width // SSD_GROUPS

    def wspec(c):
        return pl.BlockSpec((None, 1, c), lambda bi, ci: (layer, 0, 0))

    return pl.pallas_call(
        functools.partial(_ssd_kernel, n_heads=n_heads),
        grid=(b, s // L),
        in_specs=[
            pl.BlockSpec((None, L, width), lambda bi, ci: (bi, ci, cols["z"] // width)),
            pl.BlockSpec((None, L, width), lambda bi, ci: (bi, ci, cols["xs"] // width)),
            pl.BlockSpec((None, L, bcw), lambda bi, ci: (bi, ci, cols["bc"] // bcw)),
            pl.BlockSpec((None, L, LANES), lambda bi, ci: (bi, ci, cols["dt"] // LANES)),
            pl.BlockSpec((None, SSD_CONV, width), lambda bi, ci: (layer, 0, 0)),
            wspec(width),
            pl.BlockSpec((None, SSD_CONV, bcw), lambda bi, ci: (layer, 0, 0)),
            wspec(bcw),
            wspec(LANES),
            wspec(LANES),
            wspec(width),
            wspec(width),
            pl.BlockSpec((LANES, width), lambda bi, ci: (0, 0)),
        ],
        out_specs=pl.BlockSpec((None, L, width), lambda bi, ci: (bi, ci, 0)),
        out_shape=jax.ShapeDtypeStruct((b, s, width), BF16),
        scratch_shapes=[pltpu.VMEM((L + SUBLANES, width), F32),
                        pltpu.VMEM((L + SUBLANES, bcw), F32),
                        pltpu.VMEM((SSD_GROUPS, SSD_STATE, gw), F32)],
        compiler_params=_cparams(("parallel", "arbitrary")),
        name="ssd",
    )(proj3, proj3, proj3, proj3, cwx, cbx, cwb, cbb, dtb, alog, dsk, nw, e)


def _cnv_kernel(glu_ref, w_ref, b_ref, lw_ref, lb_ref, o_ref, pad):
    T = CONV_TILE
    c = pad.shape[1]

    @pl.when(pl.program_id(1) == 0)
    def _():
        pad[0:CONV_HALO, :] = jnp.zeros((CONV_HALO, c), F32)

    pad[CONV_HALO:CONV_HALO + T, :] = glu_ref[:, 0:c] * jax.nn.sigmoid(glu_ref[:, c:2 * c])
    acc = b_ref[...]
    for k in range(CONV_KERNEL):
        off = CONV_HALO - (CONV_KERNEL - 1) + k
        acc = acc + w_ref[k:k + 1, :] * pad[off:off + T, :]
    pad[0:CONV_HALO, :] = pad[T:T + CONV_HALO, :]
    mu = jnp.mean(acc, axis=-1, keepdims=True)
    cen = acc - mu
    var = jnp.mean(cen * cen, axis=-1, keepdims=True)
    v = cen * lax.rsqrt(var + EPS) * lw_ref[...] + lb_ref[...]
    o_ref[...] = _silu(v).astype(o_ref.dtype)


def _cnv(proj3, col, w, bias, lw, lb, layer):
    b, s, _ = proj3.shape
    c = w.shape[2]
    T = CONV_TILE
    return pl.pallas_call(
        _cnv_kernel,
        grid=(b, s // T),
        in_specs=[
            pl.BlockSpec((None, T, 2 * c), lambda bi, ti: (bi, ti, col // (2 * c))),
            pl.BlockSpec((None, CONV_KERNEL, c), lambda bi, ti: (layer, 0, 0)),
            pl.BlockSpec((None, 1, c), lambda bi, ti: (layer, 0, 0)),
            pl.BlockSpec((None, 1, c), lambda bi, ti: (layer, 0, 0)),
            pl.BlockSpec((None, 1, c), lambda bi, ti: (layer, 0, 0)),
        ],
        out_specs=pl.BlockSpec((None, T, c), lambda bi, ti: (bi, ti, 0)),
        out_shape=jax.ShapeDtypeStruct((b, s, c), BF16),
        scratch_shapes=[pltpu.VMEM((T + CONV_HALO, c), F32)],
        compiler_params=_cparams(("parallel", "arbitrary")),
        name="cnv",
    )(proj3, w, bias, lw, lb)


def _cmp_kernel(u_ref, pe_ref, w1_ref, b1_ref, w2_ref, w2t_ref, on_ref, ot_ref):
    u = u_ref[...]
    half = u.shape[1]
    nc = u.shape[0]
    a0 = (u + pe_ref[:, 0:half]).astype(BF16)
    a1 = (u + pe_ref[:, half:2 * half]).astype(BF16)
    h0 = _dot(a0, w1_ref[0:half, :])
    h1 = _dot(a1, w1_ref[half:2 * half, :])
    pre = h0 + pltpu.roll(h1, nc - 1, 0) + b1_ref[...]
    act = _silu(pre).astype(BF16)
    on_ref[...] = _dot(act, w2_ref[...]).astype(on_ref.dtype)
    ot_ref[...] = _dot_nt(w2t_ref[...], act).astype(ot_ref.dtype)


def _cmp(u16, pe, w1, b1, w2, w2t, layer):
    _, b, g, nc, uw = u16.shape
    hid = w1.shape[3]
    dh = w2.shape[3]
    return pl.pallas_call(
        _cmp_kernel,
        grid=(2, b, g),
        in_specs=[
            pl.BlockSpec((None, None, None, nc, uw), lambda k, bi, gi: (k, bi, gi, 0, 0)),
            pl.BlockSpec((None, None, 1, 2 * uw), lambda k, bi, gi: (k, layer, 0, 0)),
            pl.BlockSpec((None, None, 2 * uw, hid), lambda k, bi, gi: (k, layer, 0, 0)),
            pl.BlockSpec((None, None, 1, hid), lambda k, bi, gi: (k, layer, 0, 0)),
            pl.BlockSpec((None, None, hid, dh), lambda k, bi, gi: (k, layer, 0, 0)),
            pl.BlockSpec((None, None, dh, hid), lambda k, bi, gi: (k, layer, 0, 0)),
        ],
        out_specs=[
            pl.BlockSpec((None, None, None, nc, dh), lambda k, bi, gi: (k, bi, gi, 0, 0)),
            pl.BlockSpec((None, None, None, dh, nc), lambda k, bi, gi: (k, bi, gi, 0, 0)),
        ],
        out_shape=[jax.ShapeDtypeStruct((2, b, g, nc, dh), BF16),
                   jax.ShapeDtypeStruct((2, b, g, dh, nc), BF16)],
        compiler_params=_cparams(("parallel", "parallel", "parallel")),
        name="cmp",
    )(u16, pe, w1, b1, w2, w2t)


def _nsa_kernel(q_ref, gl_ref, kc_ref, vct_ref, ks_ref, vst_ref, kw_ref, vwt_ref, ovt_ref,
                o_ref, bias_scr, *, n_sel):
    Q = Q_BLOCK
    dh = NSA_HEAD_DIM
    ncol = q_ref.shape[0]
    hpg = ncol // Q
    nc = kc_ref.shape[0]
    nb = ovt_ref.shape[0]
    kt = ks_ref.shape[1]
    bpt = kt // SLC_BLOCK
    qi = pl.program_id(2)
    q0 = qi * Q

    q = q_ref[...] * (dh ** -0.5)
    lane = lax.broadcasted_iota(jnp.int32, (1, ncol), 1)
    tok = q0 + (lane & (Q - 1))

    sc = _dot_nt(kc_ref[...], q)
    ci = lax.broadcasted_iota(jnp.int32, (nc, 1), 0)
    valid_c = (ci * CMP_STRIDE + (CMP_BLOCK - 1)) <= tok
    scm = jnp.where(valid_c, sc, NEG)
    m_c = jnp.max(scm, axis=0, keepdims=True)
    p_c = jnp.where(valid_c, jnp.exp(scm - m_c), 0.0)
    l_c = jnp.sum(p_c, axis=0, keepdims=True)
    p_c = p_c * jnp.where(l_c > 0.0, 1.0 / l_c, 0.0)
    o_c = _dot(vct_ref[...], p_c.astype(BF16))

    psum = p_c[:, 0:Q]
    for h in range(1, hpg):
        psum = psum + p_c[:, h * Q:(h + 1) * Q]
    imp = _dot_exact_lhs(ovt_ref[...], psum)
    blk = lax.broadcasted_iota(jnp.int32, (nb, Q), 0)
    t1 = q0 + lax.broadcasted_iota(jnp.int32, (nb, Q), 1)
    cur = lax.shift_right_logical(t1, SLC_BLOCK.bit_length() - 1)
    forced = (blk == 0) | (blk == cur) | (blk == cur - 1)
    causal_b = blk * SLC_BLOCK <= t1
    val = jnp.where(causal_b, jnp.where(forced, imp + FORCE_BONUS, imp), NEG)
    blk_f = blk.astype(F32)
    bias = jnp.full((nb, Q), NEG, F32)
    for _ in range(n_sel):
        mx = jnp.max(val, axis=0, keepdims=True)
        idx = jnp.min(jnp.where(val == mx, blk_f, float(nb)), axis=0, keepdims=True)
        pick = blk_f == idx
        bias = jnp.where(pick, 0.0, bias)
        val = jnp.where(pick, -jnp.inf, val)
    bias = jnp.where(causal_b, bias, NEG)
    bias_scr[...] = bias.reshape(nb // bpt, bpt, Q)

    def sel_tile(j, carry, diag):
        m, l, acc = carry
        s = _dot_nt(ks_ref[j], q)
        b8 = bias_scr[j]
        b1 = jnp.concatenate(
            [jnp.broadcast_to(b8[r:r + 1, :], (SLC_BLOCK, Q)) for r in range(bpt)], axis=0)
        s = s + jnp.concatenate([b1] * hpg, axis=1)
        if diag:
            kpos = j * kt + lax.broadcasted_iota(jnp.int32, (kt, 1), 0)
            s = jnp.where(kpos <= tok, s, NEG)
        m_new = jnp.maximum(m, jnp.max(s, axis=0, keepdims=True))
        alpha = jnp.exp(m - m_new)
        p = jnp.exp(s - m_new)
        l = alpha * l + jnp.sum(p, axis=0, keepdims=True)
        acc = alpha * acc + _dot(vst_ref[j], p.astype(BF16))
        return m_new, l, acc

    jd = q0 // kt
    init = (jnp.full((1, ncol), NEG, F32), jnp.zeros((1, ncol), F32), jnp.zeros((dh, ncol), F32))
    carry = lax.fori_loop(0, jd, lambda j, c: sel_tile(j, c, False), init)
    _, l_s, acc_s = sel_tile(jd, carry, True)
    o_s = acc_s * (1.0 / l_s)

    nwin = WINDOW + Q
    kw = kw_ref[pl.ds(pl.multiple_of(q0, Q), nwin), :]
    sw = _dot_nt(kw, q)
    pos = q0 - WINDOW + lax.broadcasted_iota(jnp.int32, (nwin, 1), 0)
    dist = tok - pos
    valid_w = (dist >= 0) & (dist < WINDOW) & (pos >= 0)
    swm = jnp.where(valid_w, sw, NEG)
    m_w = jnp.max(swm, axis=0, keepdims=True)
    p_w = jnp.where(valid_w, jnp.exp(swm - m_w), 0.0)
    l_w = jnp.sum(p_w, axis=0, keepdims=True)
    p_w16 = p_w.astype(BF16)
    o_w = jnp.zeros((dh, ncol), F32)
    for c in range(nwin // Q):
        o_w = o_w + _dot(vwt_ref[qi + c], p_w16[c * Q:(c + 1) * Q, :])
    o_w = o_w * (1.0 / l_w)

    gates = jax.nn.sigmoid(gl_ref[...])
    o_t = gates[0:1, :] * o_c + gates[1:2, :] * o_s + gates[2:3, :] * o_w
    per = LANES // dh
    outs = []
    for hp in range(hpg // per):
        stack = jnp.concatenate([o_t[:, (hp * per + k) * Q:(hp * per + k + 1) * Q]
                                 for k in range(per)], axis=0)
        outs.append(stack.T)
    o_ref[...] = jnp.concatenate(outs, axis=1).astype(o_ref.dtype)


def _nsa(qr, glt, cn, ct, ks, vst, kwp, vwt, ovt, s, n_sel):
    b, g, nqb, ncol, dh = qr.shape
    nc = cn.shape[3]
    nt, kt = ks.shape[2], ks.shape[3]
    nb = ovt.shape[0]
    hpg = ncol // Q_BLOCK
    return pl.pallas_call(
        functools.partial(_nsa_kernel, n_sel=n_sel),
        grid=(b, g, nqb),
        in_specs=[
            pl.BlockSpec((None, None, None, ncol, dh), lambda bi, gi, qi: (bi, gi, qi, 0, 0)),
            pl.BlockSpec((None, None, None, 3, ncol), lambda bi, gi, qi: (bi, gi, qi, 0, 0)),
            pl.BlockSpec((None, None, None, nc, dh), lambda bi, gi, qi: (0, bi, gi, 0, 0)),
            pl.BlockSpec((None, None, None, dh, nc), lambda bi, gi, qi: (1, bi, gi, 0, 0)),
            pl.BlockSpec((None, None, nt, kt, dh), lambda bi, gi, qi: (bi, gi, 0, 0, 0)),
            pl.BlockSpec((None, None, nt, dh, kt), lambda bi, gi, qi: (bi, gi, 0, 0, 0)),
            pl.BlockSpec((None, None, s + WINDOW, dh), lambda bi, gi, qi: (bi, gi, 0, 0)),
            pl.BlockSpec((None, None, (s + WINDOW) // Q_BLOCK, dh, Q_BLOCK),
                         lambda bi, gi, qi: (bi, gi, 0, 0, 0)),
            pl.BlockSpec((nb, nc), lambda bi, gi, qi: (0, 0)),
        ],
        out_specs=pl.BlockSpec((None, Q_BLOCK, hpg * dh), lambda bi, gi, qi: (bi, qi, gi)),
        out_shape=jax.ShapeDtypeStruct((b, s, g * hpg * dh), BF16),
        scratch_shapes=[pltpu.VMEM((nb // (kt // SLC_BLOCK), kt // SLC_BLOCK, Q_BLOCK), F32)],
        compiler_params=_cparams(("parallel", "parallel", "arbitrary")),
        name="nsa",
    )(qr, glt, cn, ct, ks, vst, kwp, vwt, ovt)


def _outproj_kernel(h_ref, ys_ref, yn_ref, yc_ref, w_ref, nw_ref, o_ref):
    w0 = ys_ref.shape[1]
    w1 = w0 + yn_ref.shape[1]
    w2 = w1 + yc_ref.shape[1]
    m = (_dot(ys_ref[...], w_ref[0:w0, :]) + _dot(yn_ref[...], w_ref[w0:w1, :])
         + _dot(yc_ref[...], w_ref[w1:w2, :]))
    o_ref[...] = h_ref[...] + _rms(m, nw_ref[...])


def _outproj(h, ys, yn, yc, w, nw, layer):
    n, d = h.shape
    dm = w.shape[1]
    return pl.pallas_call(
        _outproj_kernel,
        grid=(n // TOKEN_TILE,),
        in_specs=[
            pl.BlockSpec((TOKEN_TILE, d), lambda i: (i, 0)),
            pl.BlockSpec((TOKEN_TILE, ys.shape[1]), lambda i: (i, 0)),
            pl.BlockSpec((TOKEN_TILE, yn.shape[1]), lambda i: (i, 0)),
            pl.BlockSpec((TOKEN_TILE, yc.shape[1]), lambda i: (i, 0)),
            pl.BlockSpec((None, dm, d), lambda i: (layer, 0, 0)),
            pl.BlockSpec((None, 1, d), lambda i: (layer, 0, 0)),
        ],
        out_specs=pl.BlockSpec((TOKEN_TILE, d), lambda i: (i, 0)),
        out_shape=jax.ShapeDtypeStruct((n, d), F32),
        compiler_params=_cparams(("parallel",)),
        name="outproj",
    )(h, ys, yn, yc, w, nw)


def _overlap_t(n_cmp_pad, nb):
    c0 = np.arange(n_cmp_pad)[None, :] * CMP_STRIDE
    s0 = np.arange(nb)[:, None] * SLC_BLOCK
    ov = np.minimum(c0 + CMP_BLOCK, s0 + SLC_BLOCK) - np.maximum(c0, s0)
    return jnp.asarray(np.maximum(ov, 0) / CMP_STRIDE, dtype=BF16)


def kernel(x, ffn1_pre_norm, ffn1_w_gu, ffn1_w_down, ffn1_post_norm, mix_pre_norm, w_in,
           ssd_conv_w, ssd_conv_b, ssd_dt_bias, ssd_a_log, ssd_d, ssd_norm_w,
           nsa_k_pe, nsa_k_w1, nsa_k_b1, nsa_k_w2, nsa_v_pe, nsa_v_w1, nsa_v_b1, nsa_v_w2,
           cnv_dw_w, cnv_dw_b, cnv_ln_w, cnv_ln_b, w_out, mix_post_norm,
           ffn2_pre_norm, ffn2_w_gu, ffn2_w_down, ffn2_post_norm):
    bsz, s, d = x.shape
    depth = w_in.shape[0]
    n = bsz * s
    n_heads = ssd_dt_bias.shape[1]
    ssd_w = n_heads * SSD_HEAD_DIM
    bcw = 2 * SSD_GROUPS * SSD_STATE
    cch = cnv_dw_w.shape[2]
    d_in = w_in.shape[2]
    kvw = NSA_KV_GROUPS * NSA_HEAD_DIM
    n_gate = d_in - (2 * ssd_w + bcw + n_heads + 6 * kvw + 2 * cch)
    nsa_heads = n_gate // (NSA_HEAD_DIM + 3)
    nsa_w = nsa_heads * NSA_HEAD_DIM
    G = NSA_KV_GROUPS
    hpg = nsa_heads // G
    dh = NSA_HEAD_DIM

    o_z, o_xbc, o_dt = 0, ssd_w, ssd_w + ssd_w + bcw
    o_q = o_dt + n_heads
    o_kv = o_q + nsa_w
    o_gl = o_kv + 6 * kvw
    o_glu = o_gl + 3 * nsa_heads
    src = [(o_z, ssd_w), (o_glu, 2 * cch), (o_xbc, ssd_w), (o_xbc + ssd_w, bcw), (o_q, nsa_w),
           (o_kv, 6 * kvw), (o_dt, n_heads), (o_gl, 3 * nsa_heads)]
    names = ["z", "glu", "xs", "bc", "q", "kv", "dt", "gl"]
    cols, pos, pieces = {}, 0, []
    for name, (o, w) in zip(names, src):
        if name == "dt":
            assert pos % LANES == 0
        cols[name] = pos
        pieces.append(w_in[:, :, o:o + w])
        pos += w
    col_tile = 13 * LANES
    total = -(-pos // col_tile) * col_tile
    pieces.append(jnp.zeros((depth, d, total - pos), w_in.dtype))
    w_in_r = jnp.concatenate(pieces, axis=2).astype(BF16)
    assert cols["z"] % ssd_w == 0 and cols["xs"] % ssd_w == 0 and cols["bc"] % bcw == 0
    assert cols["glu"] % (2 * cch) == 0 and n_heads + 3 * nsa_heads <= LANES

    r3 = lambda a: a.reshape(depth, 1, -1)
    w1_gu, w1_dn = ffn1_w_gu.astype(BF16), ffn1_w_down.astype(BF16)
    w2_gu, w2_dn = ffn2_w_gu.astype(BF16), ffn2_w_down.astype(BF16)
    w_out16 = w_out.astype(BF16)
    cwx, cwb = ssd_conv_w[:, :, :ssd_w], ssd_conv_w[:, :, ssd_w:]
    cbx, cbb = r3(ssd_conv_b[:, :ssd_w]), r3(ssd_conv_b[:, ssd_w:])
    padl = lambda a: jnp.pad(a, ((0, 0), (0, LANES - a.shape[1]))).reshape(depth, 1, LANES)
    dtb, alog = padl(ssd_dt_bias), padl(ssd_a_log)
    dsk = r3(jnp.repeat(ssd_d, SSD_HEAD_DIM, axis=1))
    e_np = np.zeros((LANES, ssd_w), np.float32)
    for hh in range(n_heads):
        e_np[hh, hh * SSD_HEAD_DIM:(hh + 1) * SSD_HEAD_DIM] = 1.0
    e_mat = jnp.asarray(e_np, dtype=BF16)
    pe = jnp.stack([nsa_k_pe, nsa_v_pe]).reshape(2, depth, 1, CMP_BLOCK * dh)
    cw1 = jnp.stack([nsa_k_w1, nsa_v_w1]).astype(BF16)
    cb1 = jnp.stack([nsa_k_b1, nsa_v_b1]).reshape(2, depth, 1, -1)
    cw2 = jnp.stack([nsa_k_w2, nsa_v_w2]).astype(BF16)
    cw2t = jnp.swapaxes(cw2, 2, 3)
    nc = s // CMP_STRIDE
    nb = s // SLC_BLOCK
    nqb = s // Q_BLOCK
    kt = min(KV_TILE, s)
    ovt = _overlap_t(nc, nb)
    n_sel = min(SLC_TOPN, nb)

    h = x.reshape(n, d)
    for l in range(depth):
        h = _ffn(h, r3(ffn1_pre_norm), w1_gu, w1_dn, r3(ffn1_post_norm), l)

        proj = _inproj(h, r3(mix_pre_norm), w_in_r, l, col_tile)
        proj3 = proj.reshape(bsz, s, total)
        y_ssd = _ssd(proj3, cols, cwx, cbx, cwb, cbb, dtb, alog, dsk, r3(ssd_norm_w), e_mat, l, n_heads)
        y_cnv = _cnv(proj3, cols["glu"], cnv_dw_w, r3(cnv_dw_b), r3(cnv_ln_w), r3(cnv_ln_b), l)

        qr = proj3[:, :, cols["q"]:cols["q"] + nsa_w].reshape(bsz, nqb, Q_BLOCK, G, hpg, dh)
        qr = qr.transpose(0, 3, 1, 4, 2, 5).reshape(bsz, G, nqb, hpg * Q_BLOCK, dh).astype(BF16)
        glt = proj3[:, :, cols["gl"]:cols["gl"] + 3 * nsa_heads].reshape(bsz, nqb, Q_BLOCK, G, hpg, 3)
        glt = glt.transpose(0, 3, 1, 5, 4, 2).reshape(bsz, G, nqb, 3, hpg * Q_BLOCK)
        kv = proj3[:, :, cols["kv"]:cols["kv"] + 6 * kvw].reshape(bsz, s, 6, G, dh)
        kv = kv.transpose(2, 0, 3, 1, 4)
        u16 = kv[0:2].reshape(2, bsz, G, nc, CMP_STRIDE * dh)
        ks = kv[2].astype(BF16).reshape(bsz, G, s // kt, kt, dh)
        vst = kv[3].astype(BF16).reshape(bsz, G, s // kt, kt, dh).swapaxes(3, 4)
        kwp = jnp.pad(kv[4].astype(BF16), ((0, 0), (0, 0), (WINDOW, 0), (0, 0)))
        vwt = jnp.pad(kv[5].astype(BF16), ((0, 0), (0, 0), (WINDOW, 0), (0, 0)))
        vwt = vwt.reshape(bsz, G, (s + WINDOW) // Q_BLOCK, Q_BLOCK, dh).swapaxes(3, 4)

        cn, ct = _cmp(u16, pe, cw1, cb1, cw2, cw2t, l)
        y_nsa = _nsa(qr, glt, cn, ct, ks, vst, kwp, vwt, ovt, s, n_sel)

        h = _outproj(h, y_ssd.reshape(n, ssd_w), y_nsa.reshape(n, nsa_w), y_cnv.reshape(n, cch),
                     w_out16, r3(mix_post_norm), l)
        h = _ffn(h, r3(ffn2_pre_norm), w2_gu, w2_dn, r3(ffn2_post_norm), l)
    return h.reshape(bsz, s, d)
```

```python
import functools
import math

import numpy as np
import jax
import jax.numpy as jnp
from jax import lax
from jax.experimental import pallas as pl
from jax.experimental.pallas import tpu as pltpu

F32 = jnp.float32
BF16 = jnp.bfloat16

SSD_HEAD_DIM = 64
SSD_GROUPS = 2
SSD_STATE = 128
SSD_CONV = 4
SSD_CHUNK = 128
NSA_HEAD_DIM = 64
NSA_KV_GROUPS = 2
CMP_BLOCK = 32
CMP_STRIDE = 16
SLC_BLOCK = 64
SLC_TOPN = 16
WINDOW = 512
Q_BLOCK = 128
FORCE_BONUS = 1000.0
CONV_KERNEL = 31
HALF = 0.5
EPS = 1e-6
NEG = -1e30

LANES = 128
SUBLANES = 8
BF16_ROWS = 16
VMEM_LIMIT = 56 * 1024 * 1024
TOKEN_TILE = 512
FF_TILE = 512
KV_TILE = 1024
CONV_TILE = 256
CONV_HALO = 32
V_ROWS = NSA_HEAD_DIM + BF16_ROWS


def _cparams(sem):
    return pltpu.CompilerParams(dimension_semantics=sem, vmem_limit_bytes=VMEM_LIMIT)


def _rms(x, w):
    return (x * lax.rsqrt(jnp.mean(x * x, axis=-1, keepdims=True) + EPS)) * w


def _silu(x):
    return x * jax.nn.sigmoid(x)


def _dot(a, b):
    return jnp.dot(a, b, preferred_element_type=F32)


def _dot_nt(a, b):
    return lax.dot_general(a, b, (((1,), (1,)), ((), ())), preferred_element_type=F32)


def _split3(x):
    hi = x.astype(BF16)
    r1 = x - hi.astype(F32)
    mid = r1.astype(BF16)
    lo = (r1 - mid.astype(F32)).astype(BF16)
    return hi, mid, lo


def _dot_exact_lhs(a, x):
    hi, mid, lo = _split3(x)
    return _dot(a, hi) + _dot(a, mid) + _dot(a, lo)


def _ffn_kernel(h_ref, pre_ref, wg_ref, wu_ref, wd_ref, post_ref, o_ref, u_scr, acc_scr):
    j = pl.program_id(1)

    @pl.when(j == 0)
    def _():
        u_scr[...] = _rms(h_ref[...], pre_ref[...]).astype(BF16)
        acc_scr[...] = jnp.zeros_like(acc_scr)

    u = u_scr[...]
    g = _dot(u, wg_ref[...])
    v = _dot(u, wu_ref[...])
    acc_scr[...] += _dot((_silu(g) * v).astype(BF16), wd_ref[...])

    @pl.when(j == pl.num_programs(1) - 1)
    def _():
        o_ref[...] = h_ref[...] + HALF * _rms(acc_scr[...], post_ref[...])


def _ffn(h, pre_w, w_gu, w_down, post_w, layer):
    n, d = h.shape
    f = w_down.shape[1]
    nf = f // FF_TILE
    return pl.pallas_call(
        _ffn_kernel,
        grid=(n // TOKEN_TILE, nf),
        in_specs=[
            pl.BlockSpec((TOKEN_TILE, d), lambda i, j: (i, 0)),
            pl.BlockSpec((None, 1, d), lambda i, j: (layer, 0, 0)),
            pl.BlockSpec((None, d, FF_TILE), lambda i, j: (layer, 0, j)),
            pl.BlockSpec((None, d, FF_TILE), lambda i, j: (layer, 0, j + nf)),
            pl.BlockSpec((None, FF_TILE, d), lambda i, j: (layer, j, 0)),
            pl.BlockSpec((None, 1, d), lambda i, j: (layer, 0, 0)),
        ],
        out_specs=pl.BlockSpec((TOKEN_TILE, d), lambda i, j: (i, 0)),
        out_shape=jax.ShapeDtypeStruct((n, d), F32),
        scratch_shapes=[pltpu.VMEM((TOKEN_TILE, d), BF16), pltpu.VMEM((TOKEN_TILE, d), F32)],
        compiler_params=_cparams(("parallel", "arbitrary")),
        name="ffn",
    )(h, pre_w, w_gu, w_gu, w_down, post_w)


def _inproj_kernel(h_ref, nw_ref, w_ref, o_ref, u_scr):
    @pl.when(pl.program_id(1) == 0)
    def _():
        u_scr[...] = _rms(h_ref[...], nw_ref[...]).astype(BF16)

    o_ref[...] = _dot(u_scr[...], w_ref[...])


def _inproj(h, nw, w, layer, col_tile):
    n, d = h.shape
    cols = w.shape[2]
    return pl.pallas_call(
        _inproj_kernel,
        grid=(n // TOKEN_TILE, cols // col_tile),
        in_specs=[
            pl.BlockSpec((TOKEN_TILE, d), lambda i, j: (i, 0)),
            pl.BlockSpec((None, 1, d), lambda i, j: (layer, 0, 0)),
            pl.BlockSpec((None, d, col_tile), lambda i, j: (layer, 0, j)),
        ],
        out_specs=pl.BlockSpec((TOKEN_TILE, col_tile), lambda i, j: (i, j)),
        out_shape=jax.ShapeDtypeStruct((n, cols), F32),
        scratch_shapes=[pltpu.VMEM((TOKEN_TILE, d), BF16)],
        compiler_params=_cparams(("parallel", "arbitrary")),
        name="inproj",
    )(h, nw, w)


def _ssd_kernel(z_ref, xs_ref, bc_ref, dt_ref, cwx_ref, cbx_ref, cwb_ref, cbb_ref, dtb_ref,
                alog_ref, dsk_ref, nw_ref, e_ref, o_ref, xpad, bpad, hst, *, n_heads):
    L = SSD_CHUNK
    N = SSD_STATE
    P = SSD_HEAD_DIM
    G = SSD_GROUPS
    hg = n_heads // G
    gw = hg * P
    halo = SUBLANES

    @pl.when(pl.program_id(1) == 0)
    def _():
        xpad[0:halo, :] = jnp.zeros((halo, xpad.shape[1]), F32)
        bpad[0:halo, :] = jnp.zeros((halo, bpad.shape[1]), F32)
        hst[...] = jnp.zeros_like(hst)

    xpad[halo:halo + L, :] = xs_ref[...]
    bpad[halo:halo + L, :] = bc_ref[...]

    def conv_silu(pad, w_ref, b_ref):
        acc = b_ref[...]
        for k in range(SSD_CONV):
            off = halo - (SSD_CONV - 1) + k
            acc = acc + w_ref[k:k + 1, :] * pad[off:off + L, :]
        return _silu(acc)

    x = conv_silu(xpad, cwx_ref, cbx_ref)
    bcv = conv_silu(bpad, cwb_ref, cbb_ref)
    xpad[0:halo, :] = xpad[L:L + halo, :]
    bpad[0:halo, :] = bpad[L:L + halo, :]

    lane = lax.broadcasted_iota(jnp.int32, (1, LANES), 1)
    dtv = dt_ref[...] + dtb_ref[...]
    dt = jnp.maximum(dtv, 0.0) + jnp.log(1.0 + jnp.exp(-jnp.abs(dtv)))
    dt = jnp.where(lane < n_heads, dt, 0.0)
    a = -jnp.exp(alog_ref[...])
    da = dt * a
    row = lax.broadcasted_iota(jnp.int32, (L, L), 0)
    col = lax.broadcasted_iota(jnp.int32, (L, L), 1)
    tril = row >= col
    cs = _dot_exact_lhs(tril.astype(BF16), da)
    cs_t = cs.T
    dt_t = dt.T
    cs_last = cs[L - 1:L, :]
    e = e_ref[...]
    ecs = _dot(jnp.exp(cs).astype(BF16), e)
    dend = _dot((jnp.exp(cs_last - cs) * dt).astype(BF16), e)
    lane_p = lax.broadcasted_iota(jnp.int32, (L, 2 * P), 1)

    ys = []
    for g in range(G):
        bm = bcv[:, g * N:(g + 1) * N]
        cm = bcv[:, G * N + g * N:G * N + (g + 1) * N]
        bm16 = bm.astype(BF16)
        cm16 = cm.astype(BF16)
        cb = _dot_nt(cm16, bm16)
        xg = x[:, g * gw:(g + 1) * gw]
        yd = []
        for hp in range(hg // 2):
            ws = []
            for hh in range(2):
                h = g * hg + hp * 2 + hh
                seg = cs[:, h:h + 1] - cs_t[h:h + 1, :]
                dec = jnp.exp(jnp.where(tril, seg, -jnp.inf))
                ws.append(cb * dec * dt_t[h:h + 1, :])
            wcat = jnp.concatenate(ws, axis=1).astype(BF16)
            xp = xg[:, hp * 2 * P:(hp + 1) * 2 * P]
            xbd = jnp.concatenate([jnp.where(lane_p < P, xp, 0.0),
                                   jnp.where(lane_p >= P, xp, 0.0)], axis=0).astype(BF16)
            yd.append(_dot(wcat, xbd))
        y_diag = jnp.concatenate(yd, axis=1)
        hprev = hst[g]
        y_off = _dot(cm16, hprev.astype(BF16)) * ecs[:, g * gw:(g + 1) * gw]
        st = _dot(bm.T.astype(BF16), (xg * dend[:, g * gw:(g + 1) * gw]).astype(BF16))
        hst[g] = hprev * ecs[L - 1:L, g * gw:(g + 1) * gw] + st
        y = y_diag + y_off + dsk_ref[:, g * gw:(g + 1) * gw] * xg
        y = y * _silu(z_ref[:, g * gw:(g + 1) * gw])
        y = y * lax.rsqrt(jnp.mean(y * y, axis=-1, keepdims=True) + EPS)
        ys.append(y * nw_ref[:, g * gw:(g + 1) * gw])
    o_ref[...] = jnp.concatenate(ys, axis=1).astype(o_ref.dtype)


def _ssd(proj3, cols, cwx, cbx, cwb, cbb, dtb, alog, dsk, nw, e, layer, n_heads):
    b, s, _ = proj3.shape
    width = n_heads * SSD_HEAD_DIM
    bcw = 2 * SSD_GROUPS * SSD_STATE
    L = SSD_CHUNK
    gw = width // SSD_GROUPS

    def wspec(c):
        return pl.BlockSpec((None, 1, c), lambda bi, ci: (layer, 0, 0))

    return pl.pallas_call(
        functools.partial(_ssd_kernel, n_heads=n_heads),
        grid=(b, s // L),
        in_specs=[
            pl.BlockSpec((None, L, width), lambda bi, ci: (bi, ci, cols["z"] // width)),
            pl.BlockSpec((None, L, width), lambda bi, ci: (bi, ci, cols["xs"] // width)),
            pl.BlockSpec((None, L, bcw), lambda bi, ci: (bi, ci, cols["bc"] // bcw)),
            pl.BlockSpec((None, L, LANES), lambda bi, ci: (bi, ci, cols["dt"] // LANES)),
            pl.BlockSpec((None, SSD_CONV, width), lambda bi, ci: (layer, 0, 0)),
            wspec(width),
            pl.BlockSpec((None, SSD_CONV, bcw), lambda bi, ci: (layer, 0, 0)),
            wspec(bcw),
            wspec(LANES),
            wspec(LANES),
            wspec(width),
            wspec(width),
            pl.BlockSpec((LANES, width), lambda bi, ci: (0, 0)),
        ],
        out_specs=pl.BlockSpec((None, L, width), lambda bi, ci: (bi, ci, 0)),
        out_shape=jax.ShapeDtypeStruct((b, s, width), BF16),
        scratch_shapes=[pltpu.VMEM((L + SUBLANES, width), F32),
                        pltpu.VMEM((L + SUBLANES, bcw), F32),
                        pltpu.VMEM((SSD_GROUPS, SSD_STATE, gw), F32)],
        compiler_params=_cparams(("parallel", "arbitrary")),
        name="ssd",
    )(proj3, proj3, proj3, proj3, cwx, cbx, cwb, cbb, dtb, alog, dsk, nw, e)


def _cnv_kernel(glu_ref, w_ref, b_ref, lw_ref, lb_ref, o_ref, pad):
    T = CONV_TILE
    c = pad.shape[1]

    @pl.when(pl.program_id(1) == 0)
    def _():
        pad[0:CONV_HALO, :] = jnp.zeros((CONV_HALO, c), F32)

    pad[CONV_HALO:CONV_HALO + T, :] = glu_ref[:, 0:c] * jax.nn.sigmoid(glu_ref[:, c:2 * c])
    acc = b_ref[...]
    for k in range(CONV_KERNEL):
        off = CONV_HALO - (CONV_KERNEL - 1) + k
        acc = acc + w_ref[k:k + 1, :] * pad[off:off + T, :]
    pad[0:CONV_HALO, :] = pad[T:T + CONV_HALO, :]
    mu = jnp.mean(acc, axis=-1, keepdims=True)
    cen = acc - mu
    var = jnp.mean(cen * cen, axis=-1, keepdims=True)
    v = cen * lax.rsqrt(var + EPS) * lw_ref[...] + lb_ref[...]
    o_ref[...] = _silu(v).astype(o_ref.dtype)


def _cnv(proj3, col, w, bias, lw, lb, layer):
    b, s, _ = proj3.shape
    c = w.shape[2]
    T = CONV_TILE
    return pl.pallas_call(
        _cnv_kernel,
        grid=(b, s // T),
        in_specs=[
            pl.BlockSpec((None, T, 2 * c), lambda bi, ti: (bi, ti, col // (2 * c))),
            pl.BlockSpec((None, CONV_KERNEL, c), lambda bi, ti: (layer, 0, 0)),
            pl.BlockSpec((None, 1, c), lambda bi, ti: (layer, 0, 0)),
            pl.BlockSpec((None, 1, c), lambda bi, ti: (layer, 0, 0)),
            pl.BlockSpec((None, 1, c), lambda bi, ti: (layer, 0, 0)),
        ],
        out_specs=pl.BlockSpec((None, T, c), lambda bi, ti: (bi, ti, 0)),
        out_shape=jax.ShapeDtypeStruct((b, s, c), BF16),
        scratch_shapes=[pltpu.VMEM((T + CONV_HALO, c), F32)],
        compiler_params=_cparams(("parallel", "arbitrary")),
        name="cnv",
    )(proj3, w, bias, lw, lb)


def _cmp_kernel(u_ref, pe_ref, w1_ref, b1_ref, w2_ref, w2t_ref, on_ref, ot_ref):
    u = u_ref[...]
    half = u.shape[1]
    nc = u.shape[0]
    a0 = (u + pe_ref[:, 0:half]).astype(BF16)
    a1 = (u + pe_ref[:, half:2 * half]).astype(BF16)
    h0 = _dot(a0, w1_ref[0:half, :])
    h1 = _dot(a1, w1_ref[half:2 * half, :])
    pre = h0 + pltpu.roll(h1, nc - 1, 0) + b1_ref[...]
    act = _silu(pre).astype(BF16)
    on_ref[...] = _dot(act, w2_ref[...]).astype(on_ref.dtype)
    ot_ref[...] = _dot_nt(w2t_ref[...], act).astype(ot_ref.dtype)


def _cmp(u16, pe, w1, b1, w2, w2t, layer):
    _, b, g, nc, uw = u16.shape
    hid = w1.shape[3]
    dh = w2.shape[3]
    return pl.pallas_call(
        _cmp_kernel,
        grid=(2, b, g),
        in_specs=[
            pl.BlockSpec((None, None, None, nc, uw), lambda k, bi, gi: (k, bi, gi, 0, 0)),
            pl.BlockSpec((None, None, 1, 2 * uw), lambda k, bi, gi: (k, layer, 0, 0)),
            pl.BlockSpec((None, None, 2 * uw, hid), lambda k, bi, gi: (k, layer, 0, 0)),
            pl.BlockSpec((None, None, 1, hid), lambda k, bi, gi: (k, layer, 0, 0)),
            pl.BlockSpec((None, None, hid, dh), lambda k, bi, gi: (k, layer, 0, 0)),
            pl.BlockSpec((None, None, dh, hid), lambda k, bi, gi: (k, layer, 0, 0)),
        ],
        out_specs=[
            pl.BlockSpec((None, None, None, nc, dh), lambda k, bi, gi: (k, bi, gi, 0, 0)),
            pl.BlockSpec((None, None, None, dh, nc), lambda k, bi, gi: (k, bi, gi, 0, 0)),
        ],
        out_shape=[jax.ShapeDtypeStruct((2, b, g, nc, dh), BF16),
                   jax.ShapeDtypeStruct((2, b, g, dh, nc), BF16)],
        compiler_params=_cparams(("parallel", "parallel", "parallel")),
        name="cmp",
    )(u16, pe, w1, b1, w2, w2t)


def _nsa_kernel(q_ref, gl_ref, kc_ref, vct_ref, ks_ref, vst_ref, ksd_ref, vsd_ref, kw_ref,
                vwt_ref, ovt_ref, tb_ref, o_ref, s_scr, *, n_sel, nqb):
    Q = Q_BLOCK
    dh = NSA_HEAD_DIM
    ncol = q_ref.shape[0]
    hpg = ncol // Q
    nc = kc_ref.shape[0]
    nbp = ovt_ref.shape[0]
    kt = ks_ref.shape[1]
    qi = pl.program_id(2)
    q0 = qi * Q

    q = (q_ref[...] * (dh ** -0.5 * math.log2(math.e))).astype(BF16)
    lane = lax.broadcasted_iota(jnp.int32, (1, ncol), 1)
    tok = q0 + (lane & (Q - 1))

    def tile_cols(a):
        return jnp.concatenate([a] * hpg, axis=1)

    r_i = lax.broadcasted_iota(jnp.int32, (Q, Q), 0)
    c_i = lax.broadcasted_iota(jnp.int32, (Q, Q), 1)
    tri_le = jnp.where(r_i <= c_i, 0.0, NEG)
    tri_gt = jnp.where(r_i > c_i, 0.0, NEG)

    off = pl.multiple_of((nqb - 1 - qi) * (Q // CMP_STRIDE), SUBLANES)
    bias_c = tb_ref[pl.ds(off, nc), :]
    scm = _dot_nt(kc_ref[...], q) + tile_cols(bias_c)
    m_c = jnp.max(scm, axis=0, keepdims=True)
    p_c = jnp.exp2(scm - m_c)
    l_c = jnp.sum(p_c, axis=0, keepdims=True)
    p_c = p_c * jnp.where(tok >= CMP_BLOCK - 1, 1.0 / l_c, 0.0)
    o_c = _dot(vct_ref[...], p_c.astype(BF16))

    psum = p_c[:, 0:Q]
    for h in range(1, hpg):
        psum = psum + p_c[:, h * Q:(h + 1) * Q]
    imp = _dot_exact_lhs(ovt_ref[...], psum)

    nwin = WINDOW + Q
    tri_d = tile_cols(tri_le)
    qw = jnp.concatenate([jnp.full((ncol, LANES), NEG, BF16), q], axis=1)
    kw = kw_ref[pl.ds(pl.multiple_of(q0, Q), nwin), :]
    sw = _dot_nt(kw, qw)
    sw = jnp.concatenate([sw[0:Q] + tile_cols(tri_gt), sw[Q:nwin - Q],
                          sw[nwin - Q:nwin] + tri_d], axis=0)
    m_w = jnp.max(sw, axis=0, keepdims=True)
    p_w = jnp.exp2(sw - m_w).astype(BF16)
    acc_w = jnp.zeros((V_ROWS, ncol), F32)
    for c in range(nwin // Q):
        acc_w = acc_w + _dot(vwt_ref[qi + c], p_w[c * Q:(c + 1) * Q, :])
    o_w = acc_w[0:dh, :] * (1.0 / acc_w[dh:dh + 1, :])
    gates = jax.nn.sigmoid(gl_ref[...])
    o_cw = gates[0:1, :] * o_c + gates[2:3, :] * o_w

    s_d = _dot_nt(ksd_ref[...], q) + tri_d
    m_d = jnp.max(s_d, axis=0, keepdims=True)

    blk = lax.broadcasted_iota(jnp.int32, (nbp, Q), 0)
    t1 = q0 + lax.broadcasted_iota(jnp.int32, (nbp, Q), 1)
    cur = lax.shift_right_logical(t1, SLC_BLOCK.bit_length() - 1)
    forced = (blk == 0) | (blk == cur) | (blk == cur - 1)
    val = jnp.where(blk * SLC_BLOCK <= t1, jnp.where(forced, imp + FORCE_BONUS, imp), NEG)
    blk_f = blk.astype(F32)
    bias = jnp.full((nbp, Q), NEG, F32)
    for _ in range(n_sel):
        mx = jnp.max(val, axis=0, keepdims=True)
        idx = jnp.min(jnp.where(val == mx, blk_f, float(nbp)), axis=0, keepdims=True)
        pick = blk_f == idx
        bias = jnp.where(pick, 0.0, bias)
        val = jnp.where(pick, -jnp.inf, val)
    bias = jnp.where(blk < qi * (Q // SLC_BLOCK), bias, NEG)
    bias_t = tile_cols(bias).T.astype(BF16)
    qa = jnp.concatenate([bias_t, q], axis=1)

    n_trip = q0 // kt + 1

    def score_trip(g, m):
        s = _dot_nt(ks_ref[g], qa)
        s_scr[g] = s
        return jnp.maximum(m, jnp.max(s, axis=0, keepdims=True))

    m_s = lax.fori_loop(0, n_trip, score_trip, m_d)

    def value_trip(g, acc):
        p = jnp.exp2(s_scr[g] - m_s).astype(BF16)
        return acc + _dot(vst_ref[g], p)

    acc_s = _dot(vsd_ref[...], jnp.exp2(s_d - m_s).astype(BF16))
    acc_s = lax.fori_loop(0, n_trip, value_trip, acc_s)
    o_s = acc_s[0:dh, :] * (1.0 / acc_s[dh:dh + 1, :])

    o_t = o_cw + gates[1:2, :] * o_s
    per = LANES // dh
    outs = []
    for hp in range(hpg // per):
        stack = jnp.concatenate([o_t[:, (hp * per + k) * Q:(hp * per + k + 1) * Q]
                                 for k in range(per)], axis=0)
        outs.append(stack.T)
    o_ref[...] = jnp.concatenate(outs, axis=1).astype(o_ref.dtype)


def _nsa(qr, glt, cn, ct, ks, vst, ksd, vsd, kwp, vwt, ovt, tb, s, n_sel):
    b, g, nqb, ncol, dh = qr.shape
    nc = cn.shape[3]
    nt, kt, ka = ks.shape[2], ks.shape[3], ks.shape[4]
    nbp = ovt.shape[0]
    hpg = ncol // Q_BLOCK
    vr = vst.shape[3]
    once = pl.Buffered(1)
    return pl.pallas_call(
        functools.partial(_nsa_kernel, n_sel=n_sel, nqb=nqb),
        grid=(b, g, nqb),
        in_specs=[
            pl.BlockSpec((None, None, None, ncol, dh), lambda bi, gi, qi: (bi, gi, qi, 0, 0)),
            pl.BlockSpec((None, None, None, 3, ncol), lambda bi, gi, qi: (bi, gi, qi, 0, 0)),
            pl.BlockSpec((None, None, None, nc, dh), lambda bi, gi, qi: (0, bi, gi, 0, 0)),
            pl.BlockSpec((None, None, None, dh, nc), lambda bi, gi, qi: (1, bi, gi, 0, 0)),
            pl.BlockSpec((None, None, nt, kt, ka), lambda bi, gi, qi: (bi, gi, 0, 0, 0),
                         pipeline_mode=once),
            pl.BlockSpec((None, None, nt, vr, kt), lambda bi, gi, qi: (bi, gi, 0, 0, 0),
                         pipeline_mode=once),
            pl.BlockSpec((None, None, None, Q_BLOCK, dh), lambda bi, gi, qi: (bi, gi, qi, 0, 0)),
            pl.BlockSpec((None, None, None, vr, Q_BLOCK), lambda bi, gi, qi: (bi, gi, qi, 0, 0)),
            pl.BlockSpec((None, None, s + WINDOW, LANES + dh), lambda bi, gi, qi: (bi, gi, 0, 0),
                         pipeline_mode=once),
            pl.BlockSpec((None, None, (s + WINDOW) // Q_BLOCK, vr, Q_BLOCK),
                         lambda bi, gi, qi: (bi, gi, 0, 0, 0), pipeline_mode=once),
            pl.BlockSpec((nbp, nc), lambda bi, gi, qi: (0, 0)),
            pl.BlockSpec(tb.shape, lambda bi, gi, qi: (0, 0)),
        ],
        out_specs=pl.BlockSpec((None, Q_BLOCK, hpg * dh), lambda bi, gi, qi: (bi, qi, gi)),
        out_shape=jax.ShapeDtypeStruct((b, s, g * hpg * dh), BF16),
        scratch_shapes=[pltpu.VMEM((nt, kt, ncol), F32)],
        compiler_params=_cparams(("parallel", "parallel", "arbitrary")),
        name="nsa",
    )(qr, glt, cn, ct, ks, vst, ksd, vsd, kwp, vwt, ovt, tb)


def _outproj_kernel(h_ref, ys_ref, yn_ref, yc_ref, w_ref, nw_ref, o_ref):
    w0 = ys_ref.shape[1]
    w1 = w0 + yn_ref.shape[1]
    w2 = w1 + yc_ref.shape[1]
    m = (_dot(ys_ref[...], w_ref[0:w0, :]) + _dot(yn_ref[...], w_ref[w0:w1, :])
         + _dot(yc_ref[...], w_ref[w1:w2, :]))
    o_ref[...] = h_ref[...] + _rms(m, nw_ref[...])


def _outproj(h, ys, yn, yc, w, nw, layer):
    n, d = h.shape
    dm = w.shape[1]
    return pl.pallas_call(
        _outproj_kernel,
        grid=(n // TOKEN_TILE,),
        in_specs=[
            pl.BlockSpec((TOKEN_TILE, d), lambda i: (i, 0)),
            pl.BlockSpec((TOKEN_TILE, ys.shape[1]), lambda i: (i, 0)),
            pl.BlockSpec((TOKEN_TILE, yn.shape[1]), lambda i: (i, 0)),
            pl.BlockSpec((TOKEN_TILE, yc.shape[1]), lambda i: (i, 0)),
            pl.BlockSpec((None, dm, d), lambda i: (layer, 0, 0)),
            pl.BlockSpec((None, 1, d), lambda i: (layer, 0, 0)),
        ],
        out_specs=pl.BlockSpec((TOKEN_TILE, d), lambda i: (i, 0)),
        out_shape=jax.ShapeDtypeStruct((n, d), F32),
        compiler_params=_cparams(("parallel",)),
        name="outproj",
    )(h, ys, yn, yc, w, nw)


def _nsa_constants(s, nc, nb, nbp, kt, nqb):
    c0 = np.arange(nc)[None, :] * CMP_STRIDE
    s0 = np.arange(nbp)[:, None] * SLC_BLOCK
    ov = np.maximum(np.minimum(c0 + CMP_BLOCK, s0 + SLC_BLOCK) - np.maximum(c0, s0), 0) / CMP_STRIDE
    ov[nb:] = 0.0
    upq = Q_BLOCK // CMP_STRIDE
    r = np.arange(nc + upq * (nqb - 1))[:, None]
    tl = np.arange(Q_BLOCK)[None, :]
    tb = np.where(CMP_STRIDE * r + (CMP_BLOCK - 1) - Q_BLOCK * (nqb - 1) <= tl, 0.0, NEG)
    nt = s // kt
    onehot = np.zeros((nt, kt, nbp), np.float32)
    kk = np.arange(kt)
    for j in range(nt):
        onehot[j, kk, j * (kt // SLC_BLOCK) + kk // SLC_BLOCK] = 1.0
    padrow = np.zeros((s + WINDOW, LANES), np.float32)
    padrow[:WINDOW, 0] = 1.0
    vtail = np.zeros((V_ROWS - NSA_HEAD_DIM, 1), np.float32)
    vtail[0, 0] = 1.0
    return (jnp.asarray(ov, BF16), jnp.asarray(tb, F32), jnp.asarray(onehot, BF16),
            jnp.asarray(padrow, BF16), jnp.asarray(vtail, BF16))


def kernel(x, ffn1_pre_norm, ffn1_w_gu, ffn1_w_down, ffn1_post_norm, mix_pre_norm, w_in,
           ssd_conv_w, ssd_conv_b, ssd_dt_bias, ssd_a_log, ssd_d, ssd_norm_w,
           nsa_k_pe, nsa_k_w1, nsa_k_b1, nsa_k_w2, nsa_v_pe, nsa_v_w1, nsa_v_b1, nsa_v_w2,
           cnv_dw_w, cnv_dw_b, cnv_ln_w, cnv_ln_b, w_out, mix_post_norm,
           ffn2_pre_norm, ffn2_w_gu, ffn2_w_down, ffn2_post_norm):
    bsz, s, d = x.shape
    depth = w_in.shape[0]
    n = bsz * s
    n_heads = ssd_dt_bias.shape[1]
    ssd_w = n_heads * SSD_HEAD_DIM
    bcw = 2 * SSD_GROUPS * SSD_STATE
    cch = cnv_dw_w.shape[2]
    d_in = w_in.shape[2]
    kvw = NSA_KV_GROUPS * NSA_HEAD_DIM
    n_gate = d_in - (2 * ssd_w + bcw + n_heads + 6 * kvw + 2 * cch)
    nsa_heads = n_gate // (NSA_HEAD_DIM + 3)
    nsa_w = nsa_heads * NSA_HEAD_DIM
    G = NSA_KV_GROUPS
    hpg = nsa_heads // G
    dh = NSA_HEAD_DIM

    o_z, o_xbc, o_dt = 0, ssd_w, ssd_w + ssd_w + bcw
    o_q = o_dt + n_heads
    o_kv = o_q + nsa_w
    o_gl = o_kv + 6 * kvw
    o_glu = o_gl + 3 * nsa_heads
    src = [(o_z, ssd_w), (o_glu, 2 * cch), (o_xbc, ssd_w), (o_xbc + ssd_w, bcw), (o_q, nsa_w),
           (o_kv, 6 * kvw), (o_dt, n_heads), (o_gl, 3 * nsa_heads)]
    names = ["z", "glu", "xs", "bc", "q", "kv", "dt", "gl"]
    cols, pos, pieces = {}, 0, []
    for name, (o, w) in zip(names, src):
        if name == "dt":
            assert pos % LANES == 0
        cols[name] = pos
        pieces.append(w_in[:, :, o:o + w])
        pos += w
    col_tile = 13 * LANES
    total = -(-pos // col_tile) * col_tile
    pieces.append(jnp.zeros((depth, d, total - pos), w_in.dtype))
    w_in_r = jnp.concatenate(pieces, axis=2).astype(BF16)
    assert cols["z"] % ssd_w == 0 and cols["xs"] % ssd_w == 0 and cols["bc"] % bcw == 0
    assert cols["glu"] % (2 * cch) == 0 and n_heads + 3 * nsa_heads <= LANES

    r3 = lambda a: a.reshape(depth, 1, -1)
    w1_gu, w1_dn = ffn1_w_gu.astype(BF16), ffn1_w_down.astype(BF16)
    w2_gu, w2_dn = ffn2_w_gu.astype(BF16), ffn2_w_down.astype(BF16)
    w_out16 = w_out.astype(BF16)
    cwx, cwb = ssd_conv_w[:, :, :ssd_w], ssd_conv_w[:, :, ssd_w:]
    cbx, cbb = r3(ssd_conv_b[:, :ssd_w]), r3(ssd_conv_b[:, ssd_w:])
    padl = lambda a: jnp.pad(a, ((0, 0), (0, LANES - a.shape[1]))).reshape(depth, 1, LANES)
    dtb, alog = padl(ssd_dt_bias), padl(ssd_a_log)
    dsk = r3(jnp.repeat(ssd_d, SSD_HEAD_DIM, axis=1))
    e_np = np.zeros((LANES, ssd_w), np.float32)
    for hh in range(n_heads):
        e_np[hh, hh * SSD_HEAD_DIM:(hh + 1) * SSD_HEAD_DIM] = 1.0
    e_mat = jnp.asarray(e_np, dtype=BF16)
    pe = jnp.stack([nsa_k_pe, nsa_v_pe]).reshape(2, depth, 1, CMP_BLOCK * dh)
    cw1 = jnp.stack([nsa_k_w1, nsa_v_w1]).astype(BF16)
    cb1 = jnp.stack([nsa_k_b1, nsa_v_b1]).reshape(2, depth, 1, -1)
    cw2 = jnp.stack([nsa_k_w2, nsa_v_w2]).astype(BF16)
    cw2t = jnp.swapaxes(cw2, 2, 3)
    nc = s // CMP_STRIDE
    nb = s // SLC_BLOCK
    nbp = -(-nb // LANES) * LANES
    nqb = s // Q_BLOCK
    kt = min(KV_TILE, s)
    nt = s // kt
    n_sel = min(SLC_TOPN, nb)
    ovt, tb, onehot, padrow, vtail = _nsa_constants(s, nc, nb, nbp, kt, nqb)

    def v_aug(v_t):
        tail = jnp.broadcast_to(vtail, v_t.shape[:-2] + (V_ROWS - dh, v_t.shape[-1]))
        return jnp.concatenate([v_t, tail], axis=-2)

    def nsa_branch(proj3, l):
        qr = proj3[:, :, cols["q"]:cols["q"] + nsa_w].reshape(bsz, nqb, Q_BLOCK, G, hpg, dh)
        qr = qr.transpose(0, 3, 1, 4, 2, 5).reshape(bsz, G, nqb, hpg * Q_BLOCK, dh)
        glt = proj3[:, :, cols["gl"]:cols["gl"] + 3 * nsa_heads].reshape(bsz, nqb, Q_BLOCK, G, hpg, 3)
        glt = glt.transpose(0, 3, 1, 5, 4, 2).reshape(bsz, G, nqb, 3, hpg * Q_BLOCK)
        kv = proj3[:, :, cols["kv"]:cols["kv"] + 6 * kvw].reshape(bsz, s, 6, G, dh)
        kv = kv.transpose(2, 0, 3, 1, 4)
        u16 = kv[0:2].reshape(2, bsz, G, nc, CMP_STRIDE * dh)
        kv16 = kv[2:6].astype(BF16)
        ks_t = kv16[0].reshape(bsz, G, nt, kt, dh)
        ks = jnp.concatenate([jnp.broadcast_to(onehot, (bsz, G, nt, kt, nbp)), ks_t], axis=-1)
        vst = v_aug(kv16[1].reshape(bsz, G, nt, kt, dh).swapaxes(3, 4))
        ksd = kv16[0].reshape(bsz, G, nqb, Q_BLOCK, dh)
        vsd = v_aug(kv16[1].reshape(bsz, G, nqb, Q_BLOCK, dh).swapaxes(3, 4))
        zpad = ((0, 0), (0, 0), (WINDOW, 0), (0, 0))
        kwp = jnp.concatenate([jnp.broadcast_to(padrow, (bsz, G, s + WINDOW, LANES)),
                               jnp.pad(kv16[2], zpad)], axis=-1)
        vwt = jnp.pad(kv16[3], zpad).reshape(bsz, G, (s + WINDOW) // Q_BLOCK, Q_BLOCK, dh)
        vwt = v_aug(vwt.swapaxes(3, 4))

        cn, ct = _cmp(u16, pe, cw1, cb1, cw2, cw2t, l)
        return _nsa(qr, glt, cn, ct, ks, vst, ksd, vsd, kwp, vwt, ovt, tb, s, n_sel)

    h = x.reshape(n, d)
    for l in range(depth):
        h = _ffn(h, r3(ffn1_pre_norm), w1_gu, w1_dn, r3(ffn1_post_norm), l)

        proj = _inproj(h, r3(mix_pre_norm), w_in_r, l, col_tile)
        proj3 = proj.reshape(bsz, s, total)
        y_ssd = _ssd(proj3, cols, cwx, cbx, cwb, cbb, dtb, alog, dsk, r3(ssd_norm_w), e_mat, l, n_heads)
        y_cnv = _cnv(proj3, cols["glu"], cnv_dw_w, r3(cnv_dw_b), r3(cnv_ln_w), r3(cnv_ln_b), l)
        y_nsa = nsa_branch(proj3, l)

        h = _outproj(h, y_ssd.reshape(n, ssd_w), y_nsa.reshape(n, nsa_w), y_cnv.reshape(n, cch),
                     w_out16, r3(mix_post_norm), l)
        h = _ffn(h, r3(ffn2_pre_norm), w2_gu, w2_dn, r3(ffn2_post_norm), l)
    return h.reshape(bsz, s, d)
```

```python
import functools
import math

import numpy as np
import jax
import jax.numpy as jnp
from jax import lax
from jax.experimental import pallas as pl
from jax.experimental.pallas import tpu as pltpu

F32 = jnp.float32
BF16 = jnp.bfloat16

SSD_HEAD_DIM = 64
SSD_GROUPS = 2
SSD_STATE = 128
SSD_CONV = 4
SSD_CHUNK = 128
NSA_HEAD_DIM = 64
NSA_KV_GROUPS = 2
CMP_BLOCK = 32
CMP_STRIDE = 16
SLC_BLOCK = 64
SLC_TOPN = 16
WINDOW = 512
Q_BLOCK = 128
FORCE_BONUS = 1000.0
CONV_KERNEL = 31
HALF = 0.5
EPS = 1e-6
NEG = -1e30

LANES = 128
SUBLANES = 8
BF16_ROWS = 16
VMEM_LIMIT = 56 * 1024 * 1024
TOKEN_TILE = 512
FF_TILE = 512
KV_TILE = 1024
CONV_TILE = 256
CONV_HALO = 32
V_ROWS = NSA_HEAD_DIM + BF16_ROWS


def _cparams(sem):
    return pltpu.CompilerParams(dimension_semantics=sem, vmem_limit_bytes=VMEM_LIMIT)


def _rms(x, w):
    return (x * lax.rsqrt(jnp.mean(x * x, axis=-1, keepdims=True) + EPS)) * w


def _silu(x):
    return x * jax.nn.sigmoid(x)


def _dot(a, b):
    return jnp.dot(a, b, preferred_element_type=F32)


def _dot_nt(a, b):
    return lax.dot_general(a, b, (((1,), (1,)), ((), ())), preferred_element_type=F32)


def _split3(x):
    hi = x.astype(BF16)
    r1 = x - hi.astype(F32)
    mid = r1.astype(BF16)
    lo = (r1 - mid.astype(F32)).astype(BF16)
    return hi, mid, lo


def _dot_exact_lhs(a, x):
    hi, mid, lo = _split3(x)
    return _dot(a, hi) + _dot(a, mid) + _dot(a, lo)


def _ffn_kernel(h_ref, pre_ref, wg_ref, wu_ref, wd_ref, post_ref, o_ref, u_scr, acc_scr):
    j = pl.program_id(1)

    @pl.when(j == 0)
    def _():
        u_scr[...] = _rms(h_ref[...], pre_ref[...]).astype(BF16)
        acc_scr[...] = jnp.zeros_like(acc_scr)

    u = u_scr[...]
    g = _dot(u, wg_ref[...])
    v = _dot(u, wu_ref[...])
    acc_scr[...] += _dot((_silu(g) * v).astype(BF16), wd_ref[...])

    @pl.when(j == pl.num_programs(1) - 1)
    def _():
        o_ref[...] = h_ref[...] + HALF * _rms(acc_scr[...], post_ref[...])


def _ffn(h, pre_w, w_gu, w_down, post_w, layer):
    n, d = h.shape
    f = w_down.shape[1]
    nf = f // FF_TILE
    return pl.pallas_call(
        _ffn_kernel,
        grid=(n // TOKEN_TILE, nf),
        in_specs=[
            pl.BlockSpec((TOKEN_TILE, d), lambda i, j: (i, 0)),
            pl.BlockSpec((None, 1, d), lambda i, j: (layer, 0, 0)),
            pl.BlockSpec((None, d, FF_TILE), lambda i, j: (layer, 0, j)),
            pl.BlockSpec((None, d, FF_TILE), lambda i, j: (layer, 0, j + nf)),
            pl.BlockSpec((None, FF_TILE, d), lambda i, j: (layer, j, 0)),
            pl.BlockSpec((None, 1, d), lambda i, j: (layer, 0, 0)),
        ],
        out_specs=pl.BlockSpec((TOKEN_TILE, d), lambda i, j: (i, 0)),
        out_shape=jax.ShapeDtypeStruct((n, d), F32),
        scratch_shapes=[pltpu.VMEM((TOKEN_TILE, d), BF16), pltpu.VMEM((TOKEN_TILE, d), F32)],
        compiler_params=_cparams(("parallel", "arbitrary")),
        name="ffn",
    )(h, pre_w, w_gu, w_gu, w_down, post_w)


def _inproj_kernel(h_ref, nw_ref, w_ref, o_ref, u_scr):
    @pl.when(pl.program_id(1) == 0)
    def _():
        u_scr[...] = _rms(h_ref[...], nw_ref[...]).astype(BF16)

    o_ref[...] = _dot(u_scr[...], w_ref[...])


def _inproj(h, nw, w, layer, col_tile):
    n, d = h.shape
    cols = w.shape[2]
    return pl.pallas_call(
        _inproj_kernel,
        grid=(n // TOKEN_TILE, cols // col_tile),
        in_specs=[
            pl.BlockSpec((TOKEN_TILE, d), lambda i, j: (i, 0)),
            pl.BlockSpec((None, 1, d), lambda i, j: (layer, 0, 0)),
            pl.BlockSpec((None, d, col_tile), lambda i, j: (layer, 0, j)),
        ],
        out_specs=pl.BlockSpec((TOKEN_TILE, col_tile), lambda i, j: (i, j)),
        out_shape=jax.ShapeDtypeStruct((n, cols), F32),
        scratch_shapes=[pltpu.VMEM((TOKEN_TILE, d), BF16)],
        compiler_params=_cparams(("parallel", "arbitrary")),
        name="inproj",
    )(h, nw, w)


def _ssd_kernel(z_ref, xs_ref, bc_ref, dt_ref, cwx_ref, cbx_ref, cwb_ref, cbb_ref, dtb_ref,
                alog_ref, dsk_ref, nw_ref, e_ref, o_ref, xpad, bpad, hst, *, n_heads):
    L = SSD_CHUNK
    N = SSD_STATE
    P = SSD_HEAD_DIM
    G = SSD_GROUPS
    hg = n_heads // G
    gw = hg * P
    halo = SUBLANES

    @pl.when(pl.program_id(1) == 0)
    def _():
        xpad[0:halo, :] = jnp.zeros((halo, xpad.shape[1]), F32)
        bpad[0:halo, :] = jnp.zeros((halo, bpad.shape[1]), F32)
        hst[...] = jnp.zeros_like(hst)

    xpad[halo:halo + L, :] = xs_ref[...]
    bpad[halo:halo + L, :] = bc_ref[...]

    def conv_silu(pad, w_ref, b_ref):
        acc = b_ref[...]
        for k in range(SSD_CONV):
            off = halo - (SSD_CONV - 1) + k
            acc = acc + w_ref[k:k + 1, :] * pad[off:off + L, :]
        return _silu(acc)

    x = conv_silu(xpad, cwx_ref, cbx_ref)
    bcv = conv_silu(bpad, cwb_ref, cbb_ref)
    xpad[0:halo, :] = xpad[L:L + halo, :]
    bpad[0:halo, :] = bpad[L:L + halo, :]

    lane = lax.broadcasted_iota(jnp.int32, (1, LANES), 1)
    dtv = dt_ref[...] + dtb_ref[...]
    dt = jnp.maximum(dtv, 0.0) + jnp.log(1.0 + jnp.exp(-jnp.abs(dtv)))
    dt = jnp.where(lane < n_heads, dt, 0.0)
    a = -jnp.exp(alog_ref[...])
    da = dt * a
    row = lax.broadcasted_iota(jnp.int32, (L, L), 0)
    col = lax.broadcasted_iota(jnp.int32, (L, L), 1)
    tril = row >= col
    cs = _dot_exact_lhs(tril.astype(BF16), da)
    cs_t = cs.T
    dt_t = dt.T
    cs_last = cs[L - 1:L, :]
    e = e_ref[...]
    ecs = _dot(jnp.exp(cs).astype(BF16), e)
    dend = _dot((jnp.exp(cs_last - cs) * dt).astype(BF16), e)
    lane_p = lax.broadcasted_iota(jnp.int32, (L, 2 * P), 1)

    ys = []
    for g in range(G):
        bm = bcv[:, g * N:(g + 1) * N]
        cm = bcv[:, G * N + g * N:G * N + (g + 1) * N]
        bm16 = bm.astype(BF16)
        cm16 = cm.astype(BF16)
        cb = _dot_nt(cm16, bm16)
        xg = x[:, g * gw:(g + 1) * gw]
        yd = []
        for hp in range(hg // 2):
            ws = []
            for hh in range(2):
                h = g * hg + hp * 2 + hh
                seg = cs[:, h:h + 1] - cs_t[h:h + 1, :]
                dec = jnp.exp(jnp.where(tril, seg, -jnp.inf))
                ws.append(cb * dec * dt_t[h:h + 1, :])
            wcat = jnp.concatenate(ws, axis=1).astype(BF16)
            xp = xg[:, hp * 2 * P:(hp + 1) * 2 * P]
            xbd = jnp.concatenate([jnp.where(lane_p < P, xp, 0.0),
                                   jnp.where(lane_p >= P, xp, 0.0)], axis=0).astype(BF16)
            yd.append(_dot(wcat, xbd))
        y_diag = jnp.concatenate(yd, axis=1)
        hprev = hst[g]
        y_off = _dot(cm16, hprev.astype(BF16)) * ecs[:, g * gw:(g + 1) * gw]
        st = _dot(bm.T.astype(BF16), (xg * dend[:, g * gw:(g + 1) * gw]).astype(BF16))
        hst[g] = hprev * ecs[L - 1:L, g * gw:(g + 1) * gw] + st
        y = y_diag + y_off + dsk_ref[:, g * gw:(g + 1) * gw] * xg
        y = y * _silu(z_ref[:, g * gw:(g + 1) * gw])
        y = y * lax.rsqrt(jnp.mean(y * y, axis=-1, keepdims=True) + EPS)
        ys.append(y * nw_ref[:, g * gw:(g + 1) * gw])
    o_ref[...] = jnp.concatenate(ys, axis=1).astype(o_ref.dtype)


def _ssd(proj3, cols, cwx, cbx, cwb, cbb, dtb, alog, dsk, nw, e, layer, n_heads):
    b, s, _ = proj3.shape
    width = n_heads * SSD_HEAD_DIM
    bcw = 2 * SSD_GROUPS * SSD_STATE
    L = SSD_CHUNK
    gw = width // SSD_GROUPS

    def wspec(c):
        return pl.BlockSpec((None, 1, c), lambda bi, ci: (layer, 0, 0))

    return pl.pallas_call(
        functools.partial(_ssd_kernel, n_heads=n_heads),
        grid=(b, s // L),
        in_specs=[
            pl.BlockSpec((None, L, width), lambda bi, ci: (bi, ci, cols["z"] // width)),
            pl.BlockSpec((None, L, width), lambda bi, ci: (bi, ci, cols["xs"] // width)),
            pl.BlockSpec((None, L, bcw), lambda bi, ci: (bi, ci, cols["bc"] // bcw)),
            pl.BlockSpec((None, L, LANES), lambda bi, ci: (bi, ci, cols["dt"] // LANES)),
            pl.BlockSpec((None, SSD_CONV, width), lambda bi, ci: (layer, 0, 0)),
            wspec(width),
            pl.BlockSpec((None, SSD_CONV, bcw), lambda bi, ci: (layer, 0, 0)),
            wspec(bcw),
            wspec(LANES),
            wspec(LANES),
            wspec(width),
            wspec(width),
            pl.BlockSpec((LANES, width), lambda bi, ci: (0, 0)),
        ],
        out_specs=pl.BlockSpec((None, L, width), lambda bi, ci: (bi, ci, 0)),
        out_shape=jax.ShapeDtypeStruct((b, s, width), BF16),
        scratch_shapes=[pltpu.VMEM((L + SUBLANES, width), F32),
                        pltpu.VMEM((L + SUBLANES, bcw), F32),
                        pltpu.VMEM((SSD_GROUPS, SSD_STATE, gw), F32)],
        compiler_params=_cparams(("parallel", "arbitrary")),
        name="ssd",
    )(proj3, proj3, proj3, proj3, cwx, cbx, cwb, cbb, dtb, alog, dsk, nw, e)


def _cnv_kernel(glu_ref, w_ref, b_ref, lw_ref, lb_ref, o_ref, pad, shifted):
    T = CONV_TILE
    c = pad.shape[1]

    @pl.when(pl.program_id(1) == 0)
    def _():
        pad[0:CONV_HALO, :] = jnp.zeros((CONV_HALO, c), F32)

    pad[CONV_HALO:CONV_HALO + T, :] = glu_ref[:, 0:c] * jax.nn.sigmoid(glu_ref[:, c:2 * c])
    span = T + CONV_HALO - SUBLANES
    for r in range(1, SUBLANES):
        shifted[r - 1] = pad[r:r + span, :]
    acc = b_ref[...]
    for k in range(CONV_KERNEL):
        off = CONV_HALO - (CONV_KERNEL - 1) + k
        r, base = off % SUBLANES, off - off % SUBLANES
        src = pad[base:base + T, :] if r == 0 else shifted[r - 1, base:base + T, :]
        acc = acc + w_ref[k:k + 1, :] * src
    pad[0:CONV_HALO, :] = pad[T:T + CONV_HALO, :]
    mu = jnp.mean(acc, axis=-1, keepdims=True)
    cen = acc - mu
    var = jnp.mean(cen * cen, axis=-1, keepdims=True)
    v = cen * lax.rsqrt(var + EPS) * lw_ref[...] + lb_ref[...]
    o_ref[...] = _silu(v).astype(o_ref.dtype)


def _cnv(proj3, col, w, bias, lw, lb, layer):
    b, s, _ = proj3.shape
    c = w.shape[2]
    T = CONV_TILE
    return pl.pallas_call(
        _cnv_kernel,
        grid=(b, s // T),
        in_specs=[
            pl.BlockSpec((None, T, 2 * c), lambda bi, ti: (bi, ti, col // (2 * c))),
            pl.BlockSpec((None, CONV_KERNEL, c), lambda bi, ti: (layer, 0, 0)),
            pl.BlockSpec((None, 1, c), lambda bi, ti: (layer, 0, 0)),
            pl.BlockSpec((None, 1, c), lambda bi, ti: (layer, 0, 0)),
            pl.BlockSpec((None, 1, c), lambda bi, ti: (layer, 0, 0)),
        ],
        out_specs=pl.BlockSpec((None, T, c), lambda bi, ti: (bi, ti, 0)),
        out_shape=jax.ShapeDtypeStruct((b, s, c), BF16),
        scratch_shapes=[pltpu.VMEM((T + CONV_HALO, c), F32),
                        pltpu.VMEM((SUBLANES - 1, T + CONV_HALO - SUBLANES, c), F32)],
        compiler_params=_cparams(("parallel", "arbitrary")),
        name="cnv",
    )(proj3, w, bias, lw, lb)


def _cmp_kernel(u_ref, pe_ref, w1_ref, b1_ref, w2_ref, w2t_ref, on_ref, ot_ref):
    u = u_ref[...]
    half = u.shape[1]
    nc = u.shape[0]
    a0 = (u + pe_ref[:, 0:half]).astype(BF16)
    a1 = (u + pe_ref[:, half:2 * half]).astype(BF16)
    h0 = _dot(a0, w1_ref[0:half, :])
    h1 = _dot(a1, w1_ref[half:2 * half, :])
    pre = h0 + pltpu.roll(h1, nc - 1, 0) + b1_ref[...]
    act = _silu(pre).astype(BF16)
    on_ref[...] = _dot(act, w2_ref[...]).astype(on_ref.dtype)
    ot_ref[...] = _dot_nt(w2t_ref[...], act).astype(ot_ref.dtype)


def _cmp(u16, pe, w1, b1, w2, w2t, layer):
    _, b, g, nc, uw = u16.shape
    hid = w1.shape[3]
    dh = w2.shape[3]
    return pl.pallas_call(
        _cmp_kernel,
        grid=(2, b, g),
        in_specs=[
            pl.BlockSpec((None, None, None, nc, uw), lambda k, bi, gi: (k, bi, gi, 0, 0)),
            pl.BlockSpec((None, None, 1, 2 * uw), lambda k, bi, gi: (k, layer, 0, 0)),
            pl.BlockSpec((None, None, 2 * uw, hid), lambda k, bi, gi: (k, layer, 0, 0)),
            pl.BlockSpec((None, None, 1, hid), lambda k, bi, gi: (k, layer, 0, 0)),
            pl.BlockSpec((None, None, hid, dh), lambda k, bi, gi: (k, layer, 0, 0)),
            pl.BlockSpec((None, None, dh, hid), lambda k, bi, gi: (k, layer, 0, 0)),
        ],
        out_specs=[
            pl.BlockSpec((None, None, None, nc, dh), lambda k, bi, gi: (k, bi, gi, 0, 0)),
            pl.BlockSpec((None, None, None, dh, nc), lambda k, bi, gi: (k, bi, gi, 0, 0)),
        ],
        out_shape=[jax.ShapeDtypeStruct((2, b, g, nc, dh), BF16),
                   jax.ShapeDtypeStruct((2, b, g, dh, nc), BF16)],
        compiler_params=_cparams(("parallel", "parallel", "parallel")),
        name="cmp",
    )(u16, pe, w1, b1, w2, w2t)


def _nsa_kernel(q_ref, gl_ref, kc_ref, vct_ref, ks_ref, vst_ref, ksd_ref, vsd_ref, kwd_ref,
                vwd_ref, ovt_ref, tb_ref, o_ref, s_scr, *, n_sel, nqb, gate_lane):
    Q = Q_BLOCK
    dh = NSA_HEAD_DIM
    hpg = q_ref.shape[1] // dh
    ncol = hpg * Q
    nc = kc_ref.shape[0]
    nbp = ovt_ref.shape[0]
    kt = ks_ref.shape[1]
    qi = pl.program_id(2)
    q0 = qi * Q

    qt = q_ref[...] * (dh ** -0.5 * math.log2(math.e))
    q = jnp.concatenate([qt[:, h * dh:(h + 1) * dh] for h in range(hpg)], axis=0).astype(BF16)
    gl_t = gl_ref[...].T
    gates = [jax.nn.sigmoid(jnp.concatenate(
        [gl_t[gate_lane + 3 * h + k:gate_lane + 3 * h + k + 1, :] for h in range(hpg)], axis=1))
        for k in range(3)]
    lane = lax.broadcasted_iota(jnp.int32, (1, ncol), 1)
    tok = q0 + (lane & (Q - 1))

    def tile_cols(a):
        return jnp.concatenate([a] * hpg, axis=1)

    r_i = lax.broadcasted_iota(jnp.int32, (Q, Q), 0)
    c_i = lax.broadcasted_iota(jnp.int32, (Q, Q), 1)
    tri_le = jnp.where(r_i <= c_i, 0.0, NEG)
    tri_gt = jnp.where(r_i > c_i, 0.0, NEG)

    off = pl.multiple_of((nqb - 1 - qi) * (Q // CMP_STRIDE), SUBLANES)
    bias_c = tb_ref[pl.ds(off, nc), :]
    scm = _dot_nt(kc_ref[...], q) + tile_cols(bias_c)
    m_c = jnp.max(scm, axis=0, keepdims=True)
    p_c = jnp.exp2(scm - m_c)
    l_c = jnp.sum(p_c, axis=0, keepdims=True)
    p_c = p_c * jnp.where(tok >= CMP_BLOCK - 1, 1.0 / l_c, 0.0)
    o_c = _dot(vct_ref[...], p_c.astype(BF16))

    psum = p_c[:, 0:Q]
    for h in range(1, hpg):
        psum = psum + p_c[:, h * Q:(h + 1) * Q]
    imp = _dot_exact_lhs(ovt_ref[...], psum)

    nch = WINDOW // Q + 1
    tri_d = tile_cols(tri_le)
    first = qi - (nch - 1)
    idx = [jnp.maximum(first + c, 0) for c in range(nch)]
    before = [jnp.where(first + c >= 0, 0.0, NEG) for c in range(nch)]
    sw = _dot_nt(jnp.concatenate([kwd_ref[i] for i in idx], axis=0), q)
    sw = jnp.concatenate(
        [sw[0:Q] + (tile_cols(tri_gt) + before[0])]
        + [sw[c * Q:(c + 1) * Q] + before[c] for c in range(1, nch - 1)]
        + [sw[(nch - 1) * Q:nch * Q] + tri_d], axis=0)
    m_w = jnp.max(sw, axis=0, keepdims=True)
    p_w = jnp.exp2(sw - m_w).astype(BF16)
    acc_w = jnp.zeros((V_ROWS, ncol), F32)
    for c in range(nch):
        acc_w = acc_w + _dot(vwd_ref[idx[c]], p_w[c * Q:(c + 1) * Q, :])
    o_w = acc_w[0:dh, :] * (1.0 / acc_w[dh:dh + 1, :])
    o_cw = gates[0] * o_c + gates[2] * o_w

    s_d = _dot_nt(ksd_ref[...], q) + tri_d
    m_d = jnp.max(s_d, axis=0, keepdims=True)

    blk = lax.broadcasted_iota(jnp.int32, (nbp, Q), 0)
    t1 = q0 + lax.broadcasted_iota(jnp.int32, (nbp, Q), 1)
    cur = lax.shift_right_logical(t1, SLC_BLOCK.bit_length() - 1)
    forced = (blk == 0) | (blk == cur) | (blk == cur - 1)
    val = jnp.where(blk * SLC_BLOCK <= t1, jnp.where(forced, imp + FORCE_BONUS, imp), NEG)
    blk_f = blk.astype(F32)
    bias = jnp.full((nbp, Q), NEG, F32)
    for _ in range(n_sel):
        mx = jnp.max(val, axis=0, keepdims=True)
        idx = jnp.min(jnp.where(val == mx, blk_f, float(nbp)), axis=0, keepdims=True)
        pick = blk_f == idx
        bias = jnp.where(pick, 0.0, bias)
        val = jnp.where(pick, -jnp.inf, val)
    bias = jnp.where(blk < qi * (Q // SLC_BLOCK), bias, NEG)
    bias_t = tile_cols(bias).T.astype(BF16)
    qa = jnp.concatenate([bias_t, q], axis=1)

    n_trip = q0 // kt + 1

    def score_trip(g, m):
        s = _dot_nt(ks_ref[g], qa)
        s_scr[g] = s
        return jnp.maximum(m, jnp.max(s, axis=0, keepdims=True))

    m_s = lax.fori_loop(0, n_trip, score_trip, m_d)

    def value_trip(g, acc):
        p = jnp.exp2(s_scr[g] - m_s).astype(BF16)
        return acc + _dot(vst_ref[g], p)

    acc_s = _dot(vsd_ref[...], jnp.exp2(s_d - m_s).astype(BF16))
    acc_s = lax.fori_loop(0, n_trip, value_trip, acc_s)
    o_s = acc_s[0:dh, :] * (1.0 / acc_s[dh:dh + 1, :])

    o_t = o_cw + gates[1] * o_s
    per = LANES // dh
    outs = []
    for hp in range(hpg // per):
        stack = jnp.concatenate([o_t[:, (hp * per + k) * Q:(hp * per + k + 1) * Q]
                                 for k in range(per)], axis=0)
        outs.append(stack.T)
    o_ref[...] = jnp.concatenate(outs, axis=1).astype(o_ref.dtype)


def _nsa(proj3, q_col, gate_col, gate_lane, hpg, cn, ct, ks, vst, ksd, vsd, kwd, vwd, ovt, tb, n_sel):
    b, s, _ = proj3.shape
    g, nqb, dh = ks.shape[1], ksd.shape[2], NSA_HEAD_DIM
    nc = cn.shape[3]
    nt, kt, ka = ks.shape[2], ks.shape[3], ks.shape[4]
    nbp = ovt.shape[0]
    ncol = hpg * Q_BLOCK
    qw = hpg * dh
    vr = vst.shape[3]
    once = pl.Buffered(1)
    return pl.pallas_call(
        functools.partial(_nsa_kernel, n_sel=n_sel, nqb=nqb, gate_lane=gate_lane),
        grid=(b, g, nqb),
        in_specs=[
            pl.BlockSpec((None, Q_BLOCK, qw), lambda bi, gi, qi: (bi, qi, q_col // qw + gi)),
            pl.BlockSpec((None, Q_BLOCK, LANES), lambda bi, gi, qi: (bi, qi, gate_col // LANES + gi)),
            pl.BlockSpec((None, None, None, nc, dh), lambda bi, gi, qi: (0, bi, gi, 0, 0)),
            pl.BlockSpec((None, None, None, dh, nc), lambda bi, gi, qi: (1, bi, gi, 0, 0)),
            pl.BlockSpec((None, None, nt, kt, ka), lambda bi, gi, qi: (bi, gi, 0, 0, 0),
                         pipeline_mode=once),
            pl.BlockSpec((None, None, nt, vr, kt), lambda bi, gi, qi: (bi, gi, 0, 0, 0),
                         pipeline_mode=once),
            pl.BlockSpec((None, None, None, Q_BLOCK, dh), lambda bi, gi, qi: (bi, gi, qi, 0, 0)),
            pl.BlockSpec((None, None, None, vr, Q_BLOCK), lambda bi, gi, qi: (bi, gi, qi, 0, 0)),
            pl.BlockSpec((None, None, nqb, Q_BLOCK, dh), lambda bi, gi, qi: (bi, gi, 0, 0, 0),
                         pipeline_mode=once),
            pl.BlockSpec((None, None, nqb, vr, Q_BLOCK), lambda bi, gi, qi: (bi, gi, 0, 0, 0),
                         pipeline_mode=once),
            pl.BlockSpec((nbp, nc), lambda bi, gi, qi: (0, 0)),
            pl.BlockSpec(tb.shape, lambda bi, gi, qi: (0, 0)),
        ],
        out_specs=pl.BlockSpec((None, Q_BLOCK, qw), lambda bi, gi, qi: (bi, qi, gi)),
        out_shape=jax.ShapeDtypeStruct((b, s, g * qw), BF16),
        scratch_shapes=[pltpu.VMEM((nt, kt, ncol), F32)],
        compiler_params=_cparams(("parallel", "parallel", "arbitrary")),
        name="nsa",
    )(proj3, proj3, cn, ct, ks, vst, ksd, vsd, kwd, vwd, ovt, tb)


def _kvprep_kernel(pc_ref, ps_ref, pw_ref, craw_ref, ks_ref, vst_ref, ksd_ref, vsd_ref,
                   kwd_ref, vwd_ref):
    dh = NSA_HEAD_DIM
    Q = Q_BLOCK
    kt = ps_ref.shape[0]
    nbp = ks_ref.shape[1] - dh
    j = pl.program_id(2)
    pc = pc_ref[...]
    craw_ref[0] = pc[:, 0:dh]
    craw_ref[1] = pc[:, dh:2 * dh]

    tail_row = lax.broadcasted_iota(jnp.int32, (V_ROWS - dh, kt), 0)
    tail = jnp.where(tail_row == 0, 1.0, 0.0).astype(BF16)

    def k_and_vt(p_ref):
        p = p_ref[...]
        k = p[:, 0:dh].astype(BF16)
        vt = jnp.concatenate([p.T[dh:2 * dh, :].astype(BF16), tail], axis=0)
        return k, vt

    k_s, vt_s = k_and_vt(ps_ref)
    key = lax.broadcasted_iota(jnp.int32, (kt, nbp), 0)
    bcol = lax.broadcasted_iota(jnp.int32, (kt, nbp), 1)
    blk_of_key = j * (kt // SLC_BLOCK) + lax.shift_right_logical(key, SLC_BLOCK.bit_length() - 1)
    ks_ref[:, 0:nbp] = jnp.where(bcol == blk_of_key, 1.0, 0.0).astype(BF16)
    ks_ref[:, nbp:nbp + dh] = k_s
    vst_ref[...] = vt_s
    k_w, vt_w = k_and_vt(pw_ref)
    for c in range(kt // Q):
        ksd_ref[c] = k_s[c * Q:(c + 1) * Q, :]
        vsd_ref[c] = vt_s[:, c * Q:(c + 1) * Q]
        kwd_ref[c] = k_w[c * Q:(c + 1) * Q, :]
        vwd_ref[c] = vt_w[:, c * Q:(c + 1) * Q]


def _kvprep(proj3, kv_col, g, kt, nbp):
    b, s, _ = proj3.shape
    dh = NSA_HEAD_DIM
    nt, nqb, cpt = s // kt, s // Q_BLOCK, kt // Q_BLOCK
    pair = 2 * dh
    base = kv_col // pair

    def pspec(branch):
        return pl.BlockSpec((None, kt, pair), lambda bi, gi, j: (bi, j, base + 3 * gi + branch))

    chunk_k = pl.BlockSpec((None, None, cpt, Q_BLOCK, dh), lambda bi, gi, j: (bi, gi, j, 0, 0))
    chunk_v = pl.BlockSpec((None, None, cpt, V_ROWS, Q_BLOCK), lambda bi, gi, j: (bi, gi, j, 0, 0))
    return pl.pallas_call(
        _kvprep_kernel,
        grid=(b, g, nt),
        in_specs=[pspec(0), pspec(1), pspec(2)],
        out_specs=[
            pl.BlockSpec((2, None, None, kt, dh), lambda bi, gi, j: (0, bi, gi, j, 0)),
            pl.BlockSpec((None, None, None, kt, nbp + dh), lambda bi, gi, j: (bi, gi, j, 0, 0)),
            pl.BlockSpec((None, None, None, V_ROWS, kt), lambda bi, gi, j: (bi, gi, j, 0, 0)),
            chunk_k, chunk_v, chunk_k, chunk_v,
        ],
        out_shape=[
            jax.ShapeDtypeStruct((2, b, g, s, dh), F32),
            jax.ShapeDtypeStruct((b, g, nt, kt, nbp + dh), BF16),
            jax.ShapeDtypeStruct((b, g, nt, V_ROWS, kt), BF16),
            jax.ShapeDtypeStruct((b, g, nqb, Q_BLOCK, dh), BF16),
            jax.ShapeDtypeStruct((b, g, nqb, V_ROWS, Q_BLOCK), BF16),
            jax.ShapeDtypeStruct((b, g, nqb, Q_BLOCK, dh), BF16),
            jax.ShapeDtypeStruct((b, g, nqb, V_ROWS, Q_BLOCK), BF16),
        ],
        compiler_params=_cparams(("parallel", "parallel", "parallel")),
        name="kvprep",
    )(proj3, proj3, proj3)


def _outproj_kernel(h_ref, ys_ref, yn_ref, yc_ref, w_ref, nw_ref, o_ref):
    w0 = ys_ref.shape[1]
    w1 = w0 + yn_ref.shape[1]
    w2 = w1 + yc_ref.shape[1]
    m = (_dot(ys_ref[...], w_ref[0:w0, :]) + _dot(yn_ref[...], w_ref[w0:w1, :])
         + _dot(yc_ref[...], w_ref[w1:w2, :]))
    o_ref[...] = h_ref[...] + _rms(m, nw_ref[...])


def _outproj(h, ys, yn, yc, w, nw, layer):
    n, d = h.shape
    dm = w.shape[1]
    return pl.pallas_call(
        _outproj_kernel,
        grid=(n // TOKEN_TILE,),
        in_specs=[
            pl.BlockSpec((TOKEN_TILE, d), lambda i: (i, 0)),
            pl.BlockSpec((TOKEN_TILE, ys.shape[1]), lambda i: (i, 0)),
            pl.BlockSpec((TOKEN_TILE, yn.shape[1]), lambda i: (i, 0)),
            pl.BlockSpec((TOKEN_TILE, yc.shape[1]), lambda i: (i, 0)),
            pl.BlockSpec((None, dm, d), lambda i: (layer, 0, 0)),
            pl.BlockSpec((None, 1, d), lambda i: (layer, 0, 0)),
        ],
        out_specs=pl.BlockSpec((TOKEN_TILE, d), lambda i: (i, 0)),
        out_shape=jax.ShapeDtypeStruct((n, d), F32),
        compiler_params=_cparams(("parallel",)),
        name="outproj",
    )(h, ys, yn, yc, w, nw)


def _nsa_constants(nc, nb, nbp, nqb):
    c0 = np.arange(nc)[None, :] * CMP_STRIDE
    s0 = np.arange(nbp)[:, None] * SLC_BLOCK
    ov = np.maximum(np.minimum(c0 + CMP_BLOCK, s0 + SLC_BLOCK) - np.maximum(c0, s0), 0) / CMP_STRIDE
    ov[nb:] = 0.0
    upq = Q_BLOCK // CMP_STRIDE
    r = np.arange(nc + upq * (nqb - 1))[:, None]
    tl = np.arange(Q_BLOCK)[None, :]
    tb = np.where(CMP_STRIDE * r + (CMP_BLOCK - 1) - Q_BLOCK * (nqb - 1) <= tl, 0.0, NEG)
    return jnp.asarray(ov, BF16), jnp.asarray(tb, F32)


def kernel(x, ffn1_pre_norm, ffn1_w_gu, ffn1_w_down, ffn1_post_norm, mix_pre_norm, w_in,
           ssd_conv_w, ssd_conv_b, ssd_dt_bias, ssd_a_log, ssd_d, ssd_norm_w,
           nsa_k_pe, nsa_k_w1, nsa_k_b1, nsa_k_w2, nsa_v_pe, nsa_v_w1, nsa_v_b1, nsa_v_w2,
           cnv_dw_w, cnv_dw_b, cnv_ln_w, cnv_ln_b, w_out, mix_post_norm,
           ffn2_pre_norm, ffn2_w_gu, ffn2_w_down, ffn2_post_norm):
    bsz, s, d = x.shape
    depth = w_in.shape[0]
    n = bsz * s
    n_heads = ssd_dt_bias.shape[1]
    ssd_w = n_heads * SSD_HEAD_DIM
    bcw = 2 * SSD_GROUPS * SSD_STATE
    cch = cnv_dw_w.shape[2]
    d_in = w_in.shape[2]
    kvw = NSA_KV_GROUPS * NSA_HEAD_DIM
    n_gate = d_in - (2 * ssd_w + bcw + n_heads + 6 * kvw + 2 * cch)
    nsa_heads = n_gate // (NSA_HEAD_DIM + 3)
    nsa_w = nsa_heads * NSA_HEAD_DIM
    G = NSA_KV_GROUPS
    hpg = nsa_heads // G
    dh = NSA_HEAD_DIM

    o_z, o_xbc, o_dt = 0, ssd_w, ssd_w + ssd_w + bcw
    o_q = o_dt + n_heads
    o_kv = o_q + nsa_w
    o_gl = o_kv + 6 * kvw
    o_glu = o_gl + 3 * nsa_heads
    gl_w = 3 * hpg
    src = [("z", o_z, ssd_w), ("glu", o_glu, 2 * cch), ("xs", o_xbc, ssd_w),
           ("bc", o_xbc + ssd_w, bcw), ("q", o_q, nsa_w)]
    for gi in range(G):
        for br in range(6):
            src.append(("kv" if gi == 0 and br == 0 else None, o_kv + br * kvw + gi * dh, dh))
    cols, pos, pieces = {}, 0, []
    for name, o, w in src:
        if name:
            cols[name] = pos
        pieces.append(w_in[:, :, o:o + w].astype(BF16))
        pos += w
    zeros = lambda w: jnp.zeros((depth, d, w), BF16)
    cols["dt"] = cols["gate"] = pos
    assert pos % LANES == 0 and n_heads + gl_w <= LANES
    for gi in range(G):
        head = w_in[:, :, o_dt:o_dt + n_heads].astype(BF16) if gi == 0 else zeros(n_heads)
        pieces += [head, w_in[:, :, o_gl + gi * gl_w:o_gl + (gi + 1) * gl_w].astype(BF16),
                   zeros(LANES - n_heads - gl_w)]
        pos += LANES
    col_tile = 10 * LANES
    total = -(-pos // col_tile) * col_tile
    if total > pos:
        pieces.append(zeros(total - pos))
    w_in_r = jnp.concatenate(pieces, axis=2)
    assert cols["z"] % ssd_w == 0 and cols["xs"] % ssd_w == 0 and cols["bc"] % bcw == 0
    assert cols["glu"] % (2 * cch) == 0 and cols["q"] % (hpg * dh) == 0 and cols["kv"] % (2 * dh) == 0

    r3 = lambda a: a.reshape(depth, 1, -1)
    w1_gu, w1_dn = ffn1_w_gu.astype(BF16), ffn1_w_down.astype(BF16)
    w2_gu, w2_dn = ffn2_w_gu.astype(BF16), ffn2_w_down.astype(BF16)
    w_out16 = w_out.astype(BF16)
    cwx, cwb = ssd_conv_w[:, :, :ssd_w], ssd_conv_w[:, :, ssd_w:]
    cbx, cbb = r3(ssd_conv_b[:, :ssd_w]), r3(ssd_conv_b[:, ssd_w:])
    padl = lambda a: jnp.pad(a, ((0, 0), (0, LANES - a.shape[1]))).reshape(depth, 1, LANES)
    dtb, alog = padl(ssd_dt_bias), padl(ssd_a_log)
    dsk = r3(jnp.repeat(ssd_d, SSD_HEAD_DIM, axis=1))
    e_np = np.zeros((LANES, ssd_w), np.float32)
    for hh in range(n_heads):
        e_np[hh, hh * SSD_HEAD_DIM:(hh + 1) * SSD_HEAD_DIM] = 1.0
    e_mat = jnp.asarray(e_np, dtype=BF16)
    pe = jnp.stack([nsa_k_pe, nsa_v_pe]).reshape(2, depth, 1, CMP_BLOCK * dh)
    cw1 = jnp.stack([nsa_k_w1, nsa_v_w1]).astype(BF16)
    cb1 = jnp.stack([nsa_k_b1, nsa_v_b1]).reshape(2, depth, 1, -1)
    cw2 = jnp.stack([nsa_k_w2, nsa_v_w2]).astype(BF16)
    cw2t = jnp.swapaxes(cw2, 2, 3)
    nc = s // CMP_STRIDE
    nb = s // SLC_BLOCK
    nbp = -(-nb // LANES) * LANES
    nqb = s // Q_BLOCK
    kt = min(KV_TILE, s)
    nt = s // kt
    n_sel = min(SLC_TOPN, nb)
    ovt, tb = _nsa_constants(nc, nb, nbp, nqb)

    def nsa_branch(proj3, l):
        craw, ks, vst, ksd, vsd, kwd, vwd = _kvprep(proj3, cols["kv"], G, kt, nbp)
        u16 = craw.reshape(2, bsz, G, nc, CMP_STRIDE * dh)
        cn, ct = _cmp(u16, pe, cw1, cb1, cw2, cw2t, l)
        return _nsa(proj3, cols["q"], cols["gate"], n_heads, hpg, cn, ct, ks, vst, ksd, vsd,
                    kwd, vwd, ovt, tb, n_sel)

    h = x.reshape(n, d)
    for l in range(depth):
        h = _ffn(h, r3(ffn1_pre_norm), w1_gu, w1_dn, r3(ffn1_post_norm), l)

        proj = _inproj(h, r3(mix_pre_norm), w_in_r, l, col_tile)
        proj3 = proj.reshape(bsz, s, total)
        y_ssd = _ssd(proj3, cols, cwx, cbx, cwb, cbb, dtb, alog, dsk, r3(ssd_norm_w), e_mat, l, n_heads)
        y_cnv = _cnv(proj3, cols["glu"], cnv_dw_w, r3(cnv_dw_b), r3(cnv_ln_w), r3(cnv_ln_b), l)
        y_nsa = nsa_branch(proj3, l)

        h = _outproj(h, y_ssd.reshape(n, ssd_w), y_nsa.reshape(n, nsa_w), y_cnv.reshape(n, cch),
                     w_out16, r3(mix_post_norm), l)
        h = _ffn(h, r3(ffn2_pre_norm), w2_gu, w2_dn, r3(ffn2_post_norm), l)
    return h.reshape(bsz, s, d)
```

```python
import functools
import math

import numpy as np
import jax
import jax.numpy as jnp
from jax import lax
from jax.experimental import pallas as pl
from jax.experimental.pallas import tpu as pltpu

F32 = jnp.float32
BF16 = jnp.bfloat16

SSD_HEAD_DIM = 64
SSD_GROUPS = 2
SSD_STATE = 128
SSD_CONV = 4
SSD_CHUNK = 128
NSA_HEAD_DIM = 64
NSA_KV_GROUPS = 2
CMP_BLOCK = 32
CMP_STRIDE = 16
SLC_BLOCK = 64
SLC_TOPN = 16
WINDOW = 512
Q_BLOCK = 128
FORCE_BONUS = 1000.0
CONV_KERNEL = 31
HALF = 0.5
EPS = 1e-6
NEG = -1e30

LANES = 128
SUBLANES = 8
BF16_ROWS = 16
VMEM_LIMIT = 56 * 1024 * 1024
TOKEN_TILE = 512
PROJ_TOKEN_TILE = 1024
FF_TILE = 512
KV_TILE = 1024
CONV_TILE = 256
CONV_HALO = 32
V_ROWS = NSA_HEAD_DIM + BF16_ROWS


def _cparams(sem):
    return pltpu.CompilerParams(dimension_semantics=sem, vmem_limit_bytes=VMEM_LIMIT)


def _rms(x, w):
    return (x * lax.rsqrt(jnp.mean(x * x, axis=-1, keepdims=True) + EPS)) * w


def _silu(x):
    return x * jax.nn.sigmoid(x)


def _dot(a, b):
    return jnp.dot(a, b, preferred_element_type=F32)


def _dot_nt(a, b):
    return lax.dot_general(a, b, (((1,), (1,)), ((), ())), preferred_element_type=F32)


def _split3(x):
    hi = x.astype(BF16)
    r1 = x - hi.astype(F32)
    mid = r1.astype(BF16)
    lo = (r1 - mid.astype(F32)).astype(BF16)
    return hi, mid, lo


def _dot_exact_lhs(a, x):
    hi, mid, lo = _split3(x)
    return _dot(a, hi) + _dot(a, mid) + _dot(a, lo)


def _ffn_kernel(h_ref, pre_ref, wg_ref, wu_ref, wd_ref, post_ref, o_ref, u_scr, acc_scr):
    j = pl.program_id(1)

    @pl.when(j == 0)
    def _():
        u_scr[...] = _rms(h_ref[...], pre_ref[...]).astype(BF16)
        acc_scr[...] = jnp.zeros_like(acc_scr)

    u = u_scr[...]
    g = _dot(u, wg_ref[...])
    v = _dot(u, wu_ref[...])
    acc_scr[...] += _dot((_silu(g) * v).astype(BF16), wd_ref[...])

    @pl.when(j == pl.num_programs(1) - 1)
    def _():
        o_ref[...] = h_ref[...] + HALF * _rms(acc_scr[...], post_ref[...])


def _ffn(h, pre_w, w_gu, w_down, post_w, layer):
    n, d = h.shape
    f = w_down.shape[1]
    nf = f // FF_TILE
    return pl.pallas_call(
        _ffn_kernel,
        grid=(n // TOKEN_TILE, nf),
        in_specs=[
            pl.BlockSpec((TOKEN_TILE, d), lambda i, j: (i, 0)),
            pl.BlockSpec((None, 1, d), lambda i, j: (layer, 0, 0)),
            pl.BlockSpec((None, d, FF_TILE), lambda i, j: (layer, 0, j)),
            pl.BlockSpec((None, d, FF_TILE), lambda i, j: (layer, 0, j + nf)),
            pl.BlockSpec((None, FF_TILE, d), lambda i, j: (layer, j, 0)),
            pl.BlockSpec((None, 1, d), lambda i, j: (layer, 0, 0)),
        ],
        out_specs=pl.BlockSpec((TOKEN_TILE, d), lambda i, j: (i, 0)),
        out_shape=jax.ShapeDtypeStruct((n, d), F32),
        scratch_shapes=[pltpu.VMEM((TOKEN_TILE, d), BF16), pltpu.VMEM((TOKEN_TILE, d), F32)],
        compiler_params=_cparams(("parallel", "arbitrary")),
        name="ffn",
    )(h, pre_w, w_gu, w_gu, w_down, post_w)


def _inproj_kernel(h_ref, nw_ref, w_ref, o_ref, u_scr):
    @pl.when(pl.program_id(1) == 0)
    def _():
        u_scr[...] = _rms(h_ref[...], nw_ref[...]).astype(BF16)

    o_ref[...] = _dot(u_scr[...], w_ref[...])


def _inproj(h, nw, w, layer, col_tile):
    n, d = h.shape
    cols = w.shape[2]
    return pl.pallas_call(
        _inproj_kernel,
        grid=(n // PROJ_TOKEN_TILE, cols // col_tile),
        in_specs=[
            pl.BlockSpec((PROJ_TOKEN_TILE, d), lambda i, j: (i, 0)),
            pl.BlockSpec((None, 1, d), lambda i, j: (layer, 0, 0)),
            pl.BlockSpec((None, d, col_tile), lambda i, j: (layer, 0, j)),
        ],
        out_specs=pl.BlockSpec((PROJ_TOKEN_TILE, col_tile), lambda i, j: (i, j)),
        out_shape=jax.ShapeDtypeStruct((n, cols), F32),
        scratch_shapes=[pltpu.VMEM((PROJ_TOKEN_TILE, d), BF16)],
        compiler_params=_cparams(("parallel", "arbitrary")),
        name="inproj",
    )(h, nw, w)


def _ssd_kernel(z_ref, xs_ref, bc_ref, dt_ref, cwx_ref, cbx_ref, cwb_ref, cbb_ref, dtb_ref,
                alog_ref, dsk_ref, nw_ref, e_ref, o_ref, xpad, bpad, hst, *, n_heads):
    L = SSD_CHUNK
    N = SSD_STATE
    P = SSD_HEAD_DIM
    G = SSD_GROUPS
    hg = n_heads // G
    gw = hg * P
    halo = SUBLANES

    @pl.when(pl.program_id(1) == 0)
    def _():
        xpad[0:halo, :] = jnp.zeros((halo, xpad.shape[1]), F32)
        bpad[0:halo, :] = jnp.zeros((halo, bpad.shape[1]), F32)
        hst[...] = jnp.zeros_like(hst)

    xpad[halo:halo + L, :] = xs_ref[...]
    bpad[halo:halo + L, :] = bc_ref[...]

    def conv_silu(pad, w_ref, b_ref):
        acc = b_ref[...]
        for k in range(SSD_CONV):
            off = halo - (SSD_CONV - 1) + k
            acc = acc + w_ref[k:k + 1, :] * pad[off:off + L, :]
        return _silu(acc)

    x = conv_silu(xpad, cwx_ref, cbx_ref)
    bcv = conv_silu(bpad, cwb_ref, cbb_ref)
    xpad[0:halo, :] = xpad[L:L + halo, :]
    bpad[0:halo, :] = bpad[L:L + halo, :]

    lane = lax.broadcasted_iota(jnp.int32, (1, LANES), 1)
    dtv = dt_ref[...] + dtb_ref[...]
    dt = jnp.maximum(dtv, 0.0) + jnp.log(1.0 + jnp.exp(-jnp.abs(dtv)))
    dt = jnp.where(lane < n_heads, dt, 0.0)
    a = -jnp.exp(alog_ref[...])
    da = dt * a
    row = lax.broadcasted_iota(jnp.int32, (L, L), 0)
    col = lax.broadcasted_iota(jnp.int32, (L, L), 1)
    tril = row >= col
    cs = _dot_exact_lhs(tril.astype(BF16), da)
    cs_t = cs.T
    dt_t = dt.T
    cs_last = cs[L - 1:L, :]
    e = e_ref[...]
    ecs = _dot(jnp.exp(cs).astype(BF16), e)
    dend = _dot((jnp.exp(cs_last - cs) * dt).astype(BF16), e)
    lane_p = lax.broadcasted_iota(jnp.int32, (L, 2 * P), 1)

    ys = []
    for g in range(G):
        bm = bcv[:, g * N:(g + 1) * N]
        cm = bcv[:, G * N + g * N:G * N + (g + 1) * N]
        bm16 = bm.astype(BF16)
        cm16 = cm.astype(BF16)
        cb = _dot_nt(cm16, bm16)
        xg = x[:, g * gw:(g + 1) * gw]
        yd = []
        for hp in range(hg // 2):
            ws = []
            for hh in range(2):
                h = g * hg + hp * 2 + hh
                seg = cs[:, h:h + 1] - cs_t[h:h + 1, :]
                dec = jnp.exp(jnp.where(tril, seg, -jnp.inf))
                ws.append(cb * dec * dt_t[h:h + 1, :])
            wcat = jnp.concatenate(ws, axis=1).astype(BF16)
            xp = xg[:, hp * 2 * P:(hp + 1) * 2 * P]
            xbd = jnp.concatenate([jnp.where(lane_p < P, xp, 0.0),
                                   jnp.where(lane_p >= P, xp, 0.0)], axis=0).astype(BF16)
            yd.append(_dot(wcat, xbd))
        y_diag = jnp.concatenate(yd, axis=1)
        hprev = hst[g]
        y_off = _dot(cm16, hprev.astype(BF16)) * ecs[:, g * gw:(g + 1) * gw]
        st = _dot(bm.T.astype(BF16), (xg * dend[:, g * gw:(g + 1) * gw]).astype(BF16))
        hst[g] = hprev * ecs[L - 1:L, g * gw:(g + 1) * gw] + st
        y = y_diag + y_off + dsk_ref[:, g * gw:(g + 1) * gw] * xg
        y = y * _silu(z_ref[:, g * gw:(g + 1) * gw])
        y = y * lax.rsqrt(jnp.mean(y * y, axis=-1, keepdims=True) + EPS)
        ys.append(y * nw_ref[:, g * gw:(g + 1) * gw])
    o_ref[...] = jnp.concatenate(ys, axis=1).astype(o_ref.dtype)


def _ssd(proj3, cols, cwx, cbx, cwb, cbb, dtb, alog, dsk, nw, e, layer, n_heads):
    b, s, _ = proj3.shape
    width = n_heads * SSD_HEAD_DIM
    bcw = 2 * SSD_GROUPS * SSD_STATE
    L = SSD_CHUNK
    gw = width // SSD_GROUPS

    def wspec(c):
        return pl.BlockSpec((None, 1, c), lambda bi, ci: (layer, 0, 0))

    return pl.pallas_call(
        functools.partial(_ssd_kernel, n_heads=n_heads),
        grid=(b, s // L),
        in_specs=[
            pl.BlockSpec((None, L, width), lambda bi, ci: (bi, ci, cols["z"] // width)),
            pl.BlockSpec((None, L, width), lambda bi, ci: (bi, ci, cols["xs"] // width)),
            pl.BlockSpec((None, L, bcw), lambda bi, ci: (bi, ci, cols["bc"] // bcw)),
            pl.BlockSpec((None, L, LANES), lambda bi, ci: (bi, ci, cols["dt"] // LANES)),
            pl.BlockSpec((None, SSD_CONV, width), lambda bi, ci: (layer, 0, 0)),
            wspec(width),
            pl.BlockSpec((None, SSD_CONV, bcw), lambda bi, ci: (layer, 0, 0)),
            wspec(bcw),
            wspec(LANES),
            wspec(LANES),
            wspec(width),
            wspec(width),
            pl.BlockSpec((LANES, width), lambda bi, ci: (0, 0)),
        ],
        out_specs=pl.BlockSpec((None, L, width), lambda bi, ci: (bi, ci, 0)),
        out_shape=jax.ShapeDtypeStruct((b, s, width), BF16),
        scratch_shapes=[pltpu.VMEM((L + SUBLANES, width), F32),
                        pltpu.VMEM((L + SUBLANES, bcw), F32),
                        pltpu.VMEM((SSD_GROUPS, SSD_STATE, gw), F32)],
        compiler_params=_cparams(("parallel", "arbitrary")),
        name="ssd",
    )(proj3, proj3, proj3, proj3, cwx, cbx, cwb, cbb, dtb, alog, dsk, nw, e)


def _cnv_kernel(glu_ref, w_ref, b_ref, lw_ref, lb_ref, o_ref, pad, shifted):
    T = CONV_TILE
    c = pad.shape[1]

    @pl.when(pl.program_id(1) == 0)
    def _():
        pad[0:CONV_HALO, :] = jnp.zeros((CONV_HALO, c), F32)

    pad[CONV_HALO:CONV_HALO + T, :] = glu_ref[:, 0:c] * jax.nn.sigmoid(glu_ref[:, c:2 * c])
    span = T + CONV_HALO - SUBLANES
    for r in range(1, SUBLANES):
        shifted[r - 1] = pad[r:r + span, :]
    acc = b_ref[...]
    for k in range(CONV_KERNEL):
        off = CONV_HALO - (CONV_KERNEL - 1) + k
        r, base = off % SUBLANES, off - off % SUBLANES
        src = pad[base:base + T, :] if r == 0 else shifted[r - 1, base:base + T, :]
        acc = acc + w_ref[k:k + 1, :] * src
    pad[0:CONV_HALO, :] = pad[T:T + CONV_HALO, :]
    mu = jnp.mean(acc, axis=-1, keepdims=True)
    cen = acc - mu
    var = jnp.mean(cen * cen, axis=-1, keepdims=True)
    v = cen * lax.rsqrt(var + EPS) * lw_ref[...] + lb_ref[...]
    o_ref[...] = _silu(v).astype(o_ref.dtype)


def _cnv(proj3, col, w, bias, lw, lb, layer):
    b, s, _ = proj3.shape
    c = w.shape[2]
    T = CONV_TILE
    return pl.pallas_call(
        _cnv_kernel,
        grid=(b, s // T),
        in_specs=[
            pl.BlockSpec((None, T, 2 * c), lambda bi, ti: (bi, ti, col // (2 * c))),
            pl.BlockSpec((None, CONV_KERNEL, c), lambda bi, ti: (layer, 0, 0)),
            pl.BlockSpec((None, 1, c), lambda bi, ti: (layer, 0, 0)),
            pl.BlockSpec((None, 1, c), lambda bi, ti: (layer, 0, 0)),
            pl.BlockSpec((None, 1, c), lambda bi, ti: (layer, 0, 0)),
        ],
        out_specs=pl.BlockSpec((None, T, c), lambda bi, ti: (bi, ti, 0)),
        out_shape=jax.ShapeDtypeStruct((b, s, c), BF16),
        scratch_shapes=[pltpu.VMEM((T + CONV_HALO, c), F32),
                        pltpu.VMEM((SUBLANES - 1, T + CONV_HALO - SUBLANES, c), F32)],
        compiler_params=_cparams(("parallel", "arbitrary")),
        name="cnv",
    )(proj3, w, bias, lw, lb)


def _cmp_kernel(u_ref, pe_ref, w1_ref, b1_ref, w2_ref, w2t_ref, on_ref, ot_ref):
    u = u_ref[...]
    half = u.shape[1]
    nc = u.shape[0]
    a0 = (u + pe_ref[:, 0:half]).astype(BF16)
    a1 = (u + pe_ref[:, half:2 * half]).astype(BF16)
    h0 = _dot(a0, w1_ref[0:half, :])
    h1 = _dot(a1, w1_ref[half:2 * half, :])
    pre = h0 + pltpu.roll(h1, nc - 1, 0) + b1_ref[...]
    act = _silu(pre).astype(BF16)
    on_ref[...] = _dot(act, w2_ref[...]).astype(on_ref.dtype)
    ot_ref[...] = _dot_nt(w2t_ref[...], act).astype(ot_ref.dtype)


def _cmp(u16, pe, w1, b1, w2, w2t, layer):
    _, b, g, nc, uw = u16.shape
    hid = w1.shape[3]
    dh = w2.shape[3]
    return pl.pallas_call(
        _cmp_kernel,
        grid=(2, b, g),
        in_specs=[
            pl.BlockSpec((None, None, None, nc, uw), lambda k, bi, gi: (k, bi, gi, 0, 0)),
            pl.BlockSpec((None, None, 1, 2 * uw), lambda k, bi, gi: (k, layer, 0, 0)),
            pl.BlockSpec((None, None, 2 * uw, hid), lambda k, bi, gi: (k, layer, 0, 0)),
            pl.BlockSpec((None, None, 1, hid), lambda k, bi, gi: (k, layer, 0, 0)),
            pl.BlockSpec((None, None, hid, dh), lambda k, bi, gi: (k, layer, 0, 0)),
            pl.BlockSpec((None, None, dh, hid), lambda k, bi, gi: (k, layer, 0, 0)),
        ],
        out_specs=[
            pl.BlockSpec((None, None, None, nc, dh), lambda k, bi, gi: (k, bi, gi, 0, 0)),
            pl.BlockSpec((None, None, None, dh, nc), lambda k, bi, gi: (k, bi, gi, 0, 0)),
        ],
        out_shape=[jax.ShapeDtypeStruct((2, b, g, nc, dh), BF16),
                   jax.ShapeDtypeStruct((2, b, g, dh, nc), BF16)],
        compiler_params=_cparams(("parallel", "parallel", "parallel")),
        name="cmp",
    )(u16, pe, w1, b1, w2, w2t)


def _nsa_kernel(q_ref, gl_ref, kc_ref, vct_ref, ks_ref, vst_ref, ksd_ref, vsd_ref, kwd_ref,
                vwd_ref, ovt_ref, tb_ref, o_ref, s_scr, *, n_sel, nqb, gate_lane):
    Q = Q_BLOCK
    dh = NSA_HEAD_DIM
    hpg = q_ref.shape[1] // dh
    ncol = hpg * Q
    nc = kc_ref.shape[0]
    nbp = ovt_ref.shape[0]
    kt = ks_ref.shape[1]
    qi = pl.program_id(2)
    q0 = qi * Q

    qt = q_ref[...] * (dh ** -0.5 * math.log2(math.e))
    q = jnp.concatenate([qt[:, h * dh:(h + 1) * dh] for h in range(hpg)], axis=0).astype(BF16)
    gl_t = gl_ref[...].T
    gates = [jax.nn.sigmoid(jnp.concatenate(
        [gl_t[gate_lane + 3 * h + k:gate_lane + 3 * h + k + 1, :] for h in range(hpg)], axis=1))
        for k in range(3)]
    lane = lax.broadcasted_iota(jnp.int32, (1, ncol), 1)
    tok = q0 + (lane & (Q - 1))

    def tile_cols(a):
        return jnp.concatenate([a] * hpg, axis=1)

    r_i = lax.broadcasted_iota(jnp.int32, (Q, Q), 0)
    c_i = lax.broadcasted_iota(jnp.int32, (Q, Q), 1)
    tri_le = jnp.where(r_i <= c_i, 0.0, NEG)
    tri_gt = jnp.where(r_i > c_i, 0.0, NEG)

    off = pl.multiple_of((nqb - 1 - qi) * (Q // CMP_STRIDE), SUBLANES)
    bias_c = tb_ref[pl.ds(off, nc), :]
    scm = _dot_nt(kc_ref[...], q) + tile_cols(bias_c)
    m_c = jnp.max(scm, axis=0, keepdims=True)
    p_c = jnp.exp2(scm - m_c)
    l_c = jnp.sum(p_c, axis=0, keepdims=True)
    p_c = p_c * jnp.where(tok >= CMP_BLOCK - 1, 1.0 / l_c, 0.0)
    o_c = _dot(vct_ref[...], p_c.astype(BF16))

    psum = p_c[:, 0:Q]
    for h in range(1, hpg):
        psum = psum + p_c[:, h * Q:(h + 1) * Q]
    imp = _dot_exact_lhs(ovt_ref[...], psum)

    nch = WINDOW // Q + 1
    tri_d = tile_cols(tri_le)
    first = qi - (nch - 1)
    idx = [jnp.maximum(first + c, 0) for c in range(nch)]
    before = [jnp.where(first + c >= 0, 0.0, NEG) for c in range(nch)]
    sw = _dot_nt(jnp.concatenate([kwd_ref[i] for i in idx], axis=0), q)
    sw = jnp.concatenate(
        [sw[0:Q] + (tile_cols(tri_gt) + before[0])]
        + [sw[c * Q:(c + 1) * Q] + before[c] for c in range(1, nch - 1)]
        + [sw[(nch - 1) * Q:nch * Q] + tri_d], axis=0)
    m_w = jnp.max(sw, axis=0, keepdims=True)
    p_w = jnp.exp2(sw - m_w).astype(BF16)
    acc_w = jnp.zeros((V_ROWS, ncol), F32)
    for c in range(nch):
        acc_w = acc_w + _dot(vwd_ref[idx[c]], p_w[c * Q:(c + 1) * Q, :])
    o_w = acc_w[0:dh, :] * (1.0 / acc_w[dh:dh + 1, :])
    o_cw = gates[0] * o_c + gates[2] * o_w

    s_d = _dot_nt(ksd_ref[...], q) + tri_d
    m_d = jnp.max(s_d, axis=0, keepdims=True)

    blk = lax.broadcasted_iota(jnp.int32, (nbp, Q), 0)
    t1 = q0 + lax.broadcasted_iota(jnp.int32, (nbp, Q), 1)
    cur = lax.shift_right_logical(t1, SLC_BLOCK.bit_length() - 1)
    forced = (blk == 0) | (blk == cur) | (blk == cur - 1)
    val = jnp.where(blk * SLC_BLOCK <= t1, jnp.where(forced, imp + FORCE_BONUS, imp), NEG)
    blk_f = blk.astype(F32)
    bias = jnp.full((nbp, Q), NEG, F32)
    for _ in range(n_sel):
        mx = jnp.max(val, axis=0, keepdims=True)
        idx = jnp.min(jnp.where(val == mx, blk_f, float(nbp)), axis=0, keepdims=True)
        pick = blk_f == idx
        bias = jnp.where(pick, 0.0, bias)
        val = jnp.where(pick, -jnp.inf, val)
    bias = jnp.where(blk < qi * (Q // SLC_BLOCK), bias, NEG)
    bias_t = tile_cols(bias).T.astype(BF16)
    qa = jnp.concatenate([bias_t, q], axis=1)

    n_trip = q0 // kt + 1

    def score_trip(g, m):
        s = _dot_nt(ks_ref[g], qa)
        s_scr[g] = s
        return jnp.maximum(m, jnp.max(s, axis=0, keepdims=True))

    def paired(trip, carry):
        carry = lax.fori_loop(0, n_trip // 2,
                              lambda i, c: trip(2 * i + 1, trip(2 * i, c)), carry)
        return lax.cond(n_trip % 2 == 1, lambda c: trip(n_trip - 1, c), lambda c: c, carry)

    m_s = paired(score_trip, m_d)

    def value_trip(g, acc):
        p = jnp.exp2(s_scr[g] - m_s).astype(BF16)
        return acc + _dot(vst_ref[g], p)

    acc_s = _dot(vsd_ref[...], jnp.exp2(s_d - m_s).astype(BF16))
    acc_s = paired(value_trip, acc_s)
    o_s = acc_s[0:dh, :] * (1.0 / acc_s[dh:dh + 1, :])

    o_t = o_cw + gates[1] * o_s
    per = LANES // dh
    outs = []
    for hp in range(hpg // per):
        stack = jnp.concatenate([o_t[:, (hp * per + k) * Q:(hp * per + k + 1) * Q]
                                 for k in range(per)], axis=0)
        outs.append(stack.T)
    o_ref[...] = jnp.concatenate(outs, axis=1).astype(o_ref.dtype)


def _nsa(proj3, q_col, gate_col, gate_lane, hpg, cn, ct, ks, vst, ksd, vsd, kwd, vwd, ovt, tb, n_sel):
    b, s, _ = proj3.shape
    g, nqb, dh = ks.shape[1], ksd.shape[2], NSA_HEAD_DIM
    nc = cn.shape[3]
    nt, kt, ka = ks.shape[2], ks.shape[3], ks.shape[4]
    nbp = ovt.shape[0]
    ncol = hpg * Q_BLOCK
    qw = hpg * dh
    vr = vst.shape[3]
    once = pl.Buffered(1)
    return pl.pallas_call(
        functools.partial(_nsa_kernel, n_sel=n_sel, nqb=nqb, gate_lane=gate_lane),
        grid=(b, g, nqb),
        in_specs=[
            pl.BlockSpec((None, Q_BLOCK, qw), lambda bi, gi, qi: (bi, qi, q_col // qw + gi)),
            pl.BlockSpec((None, Q_BLOCK, LANES), lambda bi, gi, qi: (bi, qi, gate_col // LANES + gi)),
            pl.BlockSpec((None, None, None, nc, dh), lambda bi, gi, qi: (0, bi, gi, 0, 0)),
            pl.BlockSpec((None, None, None, dh, nc), lambda bi, gi, qi: (1, bi, gi, 0, 0)),
            pl.BlockSpec((None, None, nt, kt, ka), lambda bi, gi, qi: (bi, gi, 0, 0, 0),
                         pipeline_mode=once),
            pl.BlockSpec((None, None, nt, vr, kt), lambda bi, gi, qi: (bi, gi, 0, 0, 0),
                         pipeline_mode=once),
            pl.BlockSpec((None, None, None, Q_BLOCK, dh), lambda bi, gi, qi: (bi, gi, qi, 0, 0)),
            pl.BlockSpec((None, None, None, vr, Q_BLOCK), lambda bi, gi, qi: (bi, gi, qi, 0, 0)),
            pl.BlockSpec((None, None, nqb, Q_BLOCK, dh), lambda bi, gi, qi: (bi, gi, 0, 0, 0),
                         pipeline_mode=once),
            pl.BlockSpec((None, None, nqb, vr, Q_BLOCK), lambda bi, gi, qi: (bi, gi, 0, 0, 0),
                         pipeline_mode=once),
            pl.BlockSpec((nbp, nc), lambda bi, gi, qi: (0, 0)),
            pl.BlockSpec(tb.shape, lambda bi, gi, qi: (0, 0)),
        ],
        out_specs=pl.BlockSpec((None, Q_BLOCK, qw), lambda bi, gi, qi: (bi, qi, gi)),
        out_shape=jax.ShapeDtypeStruct((b, s, g * qw), BF16),
        scratch_shapes=[pltpu.VMEM((nt, kt, ncol), F32)],
        compiler_params=_cparams(("parallel", "parallel", "arbitrary")),
        name="nsa",
    )(proj3, proj3, cn, ct, ks, vst, ksd, vsd, kwd, vwd, ovt, tb)


def _kvprep_kernel(pc_ref, ps_ref, pw_ref, craw_ref, ks_ref, vst_ref, ksd_ref, vsd_ref,
                   kwd_ref, vwd_ref):
    dh = NSA_HEAD_DIM
    Q = Q_BLOCK
    kt = ps_ref.shape[0]
    nbp = ks_ref.shape[1] - dh
    j = pl.program_id(2)
    pc = pc_ref[...]
    craw_ref[0] = pc[:, 0:dh]
    craw_ref[1] = pc[:, dh:2 * dh]

    tail_row = lax.broadcasted_iota(jnp.int32, (V_ROWS - dh, kt), 0)
    tail = jnp.where(tail_row == 0, 1.0, 0.0).astype(BF16)

    def k_and_vt(p_ref):
        p = p_ref[...]
        k = p[:, 0:dh].astype(BF16)
        vt = jnp.concatenate([p.T[dh:2 * dh, :].astype(BF16), tail], axis=0)
        return k, vt

    k_s, vt_s = k_and_vt(ps_ref)
    key = lax.broadcasted_iota(jnp.int32, (kt, nbp), 0)
    bcol = lax.broadcasted_iota(jnp.int32, (kt, nbp), 1)
    blk_of_key = j * (kt // SLC_BLOCK) + lax.shift_right_logical(key, SLC_BLOCK.bit_length() - 1)
    ks_ref[:, 0:nbp] = jnp.where(bcol == blk_of_key, 1.0, 0.0).astype(BF16)
    ks_ref[:, nbp:nbp + dh] = k_s
    vst_ref[...] = vt_s
    k_w, vt_w = k_and_vt(pw_ref)
    for c in range(kt // Q):
        ksd_ref[c] = k_s[c * Q:(c + 1) * Q, :]
        vsd_ref[c] = vt_s[:, c * Q:(c + 1) * Q]
        kwd_ref[c] = k_w[c * Q:(c + 1) * Q, :]
        vwd_ref[c] = vt_w[:, c * Q:(c + 1) * Q]


def _kvprep(proj3, kv_col, g, kt, nbp):
    b, s, _ = proj3.shape
    dh = NSA_HEAD_DIM
    nt, nqb, cpt = s // kt, s // Q_BLOCK, kt // Q_BLOCK
    pair = 2 * dh
    base = kv_col // pair

    def pspec(branch):
        return pl.BlockSpec((None, kt, pair), lambda bi, gi, j: (bi, j, base + 3 * gi + branch))

    chunk_k = pl.BlockSpec((None, None, cpt, Q_BLOCK, dh), lambda bi, gi, j: (bi, gi, j, 0, 0))
    chunk_v = pl.BlockSpec((None, None, cpt, V_ROWS, Q_BLOCK), lambda bi, gi, j: (bi, gi, j, 0, 0))
    return pl.pallas_call(
        _kvprep_kernel,
        grid=(b, g, nt),
        in_specs=[pspec(0), pspec(1), pspec(2)],
        out_specs=[
            pl.BlockSpec((2, None, None, kt, dh), lambda bi, gi, j: (0, bi, gi, j, 0)),
            pl.BlockSpec((None, None, None, kt, nbp + dh), lambda bi, gi, j: (bi, gi, j, 0, 0)),
            pl.BlockSpec((None, None, None, V_ROWS, kt), lambda bi, gi, j: (bi, gi, j, 0, 0)),
            chunk_k, chunk_v, chunk_k, chunk_v,
        ],
        out_shape=[
            jax.ShapeDtypeStruct((2, b, g, s, dh), F32),
            jax.ShapeDtypeStruct((b, g, nt, kt, nbp + dh), BF16),
            jax.ShapeDtypeStruct((b, g, nt, V_ROWS, kt), BF16),
            jax.ShapeDtypeStruct((b, g, nqb, Q_BLOCK, dh), BF16),
            jax.ShapeDtypeStruct((b, g, nqb, V_ROWS, Q_BLOCK), BF16),
            jax.ShapeDtypeStruct((b, g, nqb, Q_BLOCK, dh), BF16),
            jax.ShapeDtypeStruct((b, g, nqb, V_ROWS, Q_BLOCK), BF16),
        ],
        compiler_params=_cparams(("parallel", "parallel", "parallel")),
        name="kvprep",
    )(proj3, proj3, proj3)


def _outproj_kernel(h_ref, ys_ref, yn_ref, yc_ref, w_ref, nw_ref, o_ref):
    w0 = ys_ref.shape[1]
    w1 = w0 + yn_ref.shape[1]
    w2 = w1 + yc_ref.shape[1]
    m = (_dot(ys_ref[...], w_ref[0:w0, :]) + _dot(yn_ref[...], w_ref[w0:w1, :])
         + _dot(yc_ref[...], w_ref[w1:w2, :]))
    o_ref[...] = h_ref[...] + _rms(m, nw_ref[...])


def _outproj(h, ys, yn, yc, w, nw, layer):
    n, d = h.shape
    dm = w.shape[1]
    return pl.pallas_call(
        _outproj_kernel,
        grid=(n // TOKEN_TILE,),
        in_specs=[
            pl.BlockSpec((TOKEN_TILE, d), lambda i: (i, 0)),
            pl.BlockSpec((TOKEN_TILE, ys.shape[1]), lambda i: (i, 0)),
            pl.BlockSpec((TOKEN_TILE, yn.shape[1]), lambda i: (i, 0)),
            pl.BlockSpec((TOKEN_TILE, yc.shape[1]), lambda i: (i, 0)),
            pl.BlockSpec((None, dm, d), lambda i: (layer, 0, 0)),
            pl.BlockSpec((None, 1, d), lambda i: (layer, 0, 0)),
        ],
        out_specs=pl.BlockSpec((TOKEN_TILE, d), lambda i: (i, 0)),
        out_shape=jax.ShapeDtypeStruct((n, d), F32),
        compiler_params=_cparams(("parallel",)),
        name="outproj",
    )(h, ys, yn, yc, w, nw)


def _nsa_constants(nc, nb, nbp, nqb):
    c0 = np.arange(nc)[None, :] * CMP_STRIDE
    s0 = np.arange(nbp)[:, None] * SLC_BLOCK
    ov = np.maximum(np.minimum(c0 + CMP_BLOCK, s0 + SLC_BLOCK) - np.maximum(c0, s0), 0) / CMP_STRIDE
    ov[nb:] = 0.0
    upq = Q_BLOCK // CMP_STRIDE
    r = np.arange(nc + upq * (nqb - 1))[:, None]
    tl = np.arange(Q_BLOCK)[None, :]
    tb = np.where(CMP_STRIDE * r + (CMP_BLOCK - 1) - Q_BLOCK * (nqb - 1) <= tl, 0.0, NEG)
    return jnp.asarray(ov, BF16), jnp.asarray(tb, F32)


def kernel(x, ffn1_pre_norm, ffn1_w_gu, ffn1_w_down, ffn1_post_norm, mix_pre_norm, w_in,
           ssd_conv_w, ssd_conv_b, ssd_dt_bias, ssd_a_log, ssd_d, ssd_norm_w,
           nsa_k_pe, nsa_k_w1, nsa_k_b1, nsa_k_w2, nsa_v_pe, nsa_v_w1, nsa_v_b1, nsa_v_w2,
           cnv_dw_w, cnv_dw_b, cnv_ln_w, cnv_ln_b, w_out, mix_post_norm,
           ffn2_pre_norm, ffn2_w_gu, ffn2_w_down, ffn2_post_norm):
    bsz, s, d = x.shape
    depth = w_in.shape[0]
    n = bsz * s
    n_heads = ssd_dt_bias.shape[1]
    ssd_w = n_heads * SSD_HEAD_DIM
    bcw = 2 * SSD_GROUPS * SSD_STATE
    cch = cnv_dw_w.shape[2]
    d_in = w_in.shape[2]
    kvw = NSA_KV_GROUPS * NSA_HEAD_DIM
    n_gate = d_in - (2 * ssd_w + bcw + n_heads + 6 * kvw + 2 * cch)
    nsa_heads = n_gate // (NSA_HEAD_DIM + 3)
    nsa_w = nsa_heads * NSA_HEAD_DIM
    G = NSA_KV_GROUPS
    hpg = nsa_heads // G
    dh = NSA_HEAD_DIM

    o_z, o_xbc, o_dt = 0, ssd_w, ssd_w + ssd_w + bcw
    o_q = o_dt + n_heads
    o_kv = o_q + nsa_w
    o_gl = o_kv + 6 * kvw
    o_glu = o_gl + 3 * nsa_heads
    gl_w = 3 * hpg
    src = [("z", o_z, ssd_w), ("glu", o_glu, 2 * cch), ("xs", o_xbc, ssd_w),
           ("bc", o_xbc + ssd_w, bcw), ("q", o_q, nsa_w)]
    for gi in range(G):
        for br in range(6):
            src.append(("kv" if gi == 0 and br == 0 else None, o_kv + br * kvw + gi * dh, dh))
    cols, pos, pieces = {}, 0, []
    for name, o, w in src:
        if name:
            cols[name] = pos
        pieces.append(w_in[:, :, o:o + w].astype(BF16))
        pos += w
    zeros = lambda w: jnp.zeros((depth, d, w), BF16)
    cols["dt"] = cols["gate"] = pos
    assert pos % LANES == 0 and n_heads + gl_w <= LANES
    for gi in range(G):
        head = w_in[:, :, o_dt:o_dt + n_heads].astype(BF16) if gi == 0 else zeros(n_heads)
        pieces += [head, w_in[:, :, o_gl + gi * gl_w:o_gl + (gi + 1) * gl_w].astype(BF16),
                   zeros(LANES - n_heads - gl_w)]
        pos += LANES
    col_tile = 10 * LANES
    total = -(-pos // col_tile) * col_tile
    if total > pos:
        pieces.append(zeros(total - pos))
    w_in_r = jnp.concatenate(pieces, axis=2)
    assert cols["z"] % ssd_w == 0 and cols["xs"] % ssd_w == 0 and cols["bc"] % bcw == 0
    assert cols["glu"] % (2 * cch) == 0 and cols["q"] % (hpg * dh) == 0 and cols["kv"] % (2 * dh) == 0

    r3 = lambda a: a.reshape(depth, 1, -1)
    w1_gu, w1_dn = ffn1_w_gu.astype(BF16), ffn1_w_down.astype(BF16)
    w2_gu, w2_dn = ffn2_w_gu.astype(BF16), ffn2_w_down.astype(BF16)
    w_out16 = w_out.astype(BF16)
    cwx, cwb = ssd_conv_w[:, :, :ssd_w], ssd_conv_w[:, :, ssd_w:]
    cbx, cbb = r3(ssd_conv_b[:, :ssd_w]), r3(ssd_conv_b[:, ssd_w:])
    padl = lambda a: jnp.pad(a, ((0, 0), (0, LANES - a.shape[1]))).reshape(depth, 1, LANES)
    dtb, alog = padl(ssd_dt_bias), padl(ssd_a_log)
    dsk = r3(jnp.repeat(ssd_d, SSD_HEAD_DIM, axis=1))
    e_np = np.zeros((LANES, ssd_w), np.float32)
    for hh in range(n_heads):
        e_np[hh, hh * SSD_HEAD_DIM:(hh + 1) * SSD_HEAD_DIM] = 1.0
    e_mat = jnp.asarray(e_np, dtype=BF16)
    pe = jnp.stack([nsa_k_pe, nsa_v_pe]).reshape(2, depth, 1, CMP_BLOCK * dh)
    cw1 = jnp.stack([nsa_k_w1, nsa_v_w1]).astype(BF16)
    cb1 = jnp.stack([nsa_k_b1, nsa_v_b1]).reshape(2, depth, 1, -1)
    cw2 = jnp.stack([nsa_k_w2, nsa_v_w2]).astype(BF16)
    cw2t = jnp.swapaxes(cw2, 2, 3)
    nc = s // CMP_STRIDE
    nb = s // SLC_BLOCK
    nbp = -(-nb // LANES) * LANES
    nqb = s // Q_BLOCK
    kt = min(KV_TILE, s)
    nt = s // kt
    n_sel = min(SLC_TOPN, nb)
    ovt, tb = _nsa_constants(nc, nb, nbp, nqb)

    def nsa_branch(proj3, l):
        craw, ks, vst, ksd, vsd, kwd, vwd = _kvprep(proj3, cols["kv"], G, kt, nbp)
        u16 = craw.reshape(2, bsz, G, nc, CMP_STRIDE * dh)
        cn, ct = _cmp(u16, pe, cw1, cb1, cw2, cw2t, l)
        return _nsa(proj3, cols["q"], cols["gate"], n_heads, hpg, cn, ct, ks, vst, ksd, vsd,
                    kwd, vwd, ovt, tb, n_sel)

    h = x.reshape(n, d)
    for l in range(depth):
        h = _ffn(h, r3(ffn1_pre_norm), w1_gu, w1_dn, r3(ffn1_post_norm), l)

        proj = _inproj(h, r3(mix_pre_norm), w_in_r, l, col_tile)
        proj3 = proj.reshape(bsz, s, total)
        y_ssd = _ssd(proj3, cols, cwx, cbx, cwb, cbb, dtb, alog, dsk, r3(ssd_norm_w), e_mat, l, n_heads)
        y_cnv = _cnv(proj3, cols["glu"], cnv_dw_w, r3(cnv_dw_b), r3(cnv_ln_w), r3(cnv_ln_b), l)
        y_nsa = nsa_branch(proj3, l)

        h = _outproj(h, y_ssd.reshape(n, ssd_w), y_nsa.reshape(n, nsa_w), y_cnv.reshape(n, cch),
                     w_out16, r3(mix_post_norm), l)
        h = _ffn(h, r3(ffn2_pre_norm), w2_gu, w2_dn, r3(ffn2_post_norm), l)
    return h.reshape(bsz, s, d)
```

```python
import functools
import math

import numpy as np
import jax
import jax.numpy as jnp
from jax import lax
from jax.experimental import pallas as pl
from jax.experimental.pallas import tpu as pltpu

F32 = jnp.float32
BF16 = jnp.bfloat16

SSD_HEAD_DIM = 64
SSD_GROUPS = 2
SSD_STATE = 128
SSD_CONV = 4
SSD_CHUNK = 128
NSA_HEAD_DIM = 64
NSA_KV_GROUPS = 2
CMP_BLOCK = 32
CMP_STRIDE = 16
SLC_BLOCK = 64
SLC_TOPN = 16
WINDOW = 512
Q_BLOCK = 128
FORCE_BONUS = 1000.0
CONV_KERNEL = 31
HALF = 0.5
EPS = 1e-6
NEG = -1e30

LANES = 128
SUBLANES = 8
BF16_ROWS = 16
VMEM_LIMIT = 56 * 1024 * 1024
TOKEN_TILE = 512
PROJ_TOKEN_TILE = 1024
FF_TILE = 512
KV_TILE = 1024
CONV_TILE = 256
CONV_HALO = 32
V_ROWS = NSA_HEAD_DIM + BF16_ROWS


def _cparams(sem):
    return pltpu.CompilerParams(dimension_semantics=sem, vmem_limit_bytes=VMEM_LIMIT)


def _rms(x, w):
    return (x * lax.rsqrt(jnp.mean(x * x, axis=-1, keepdims=True) + EPS)) * w


def _silu(x):
    return x * jax.nn.sigmoid(x)


def _dot(a, b):
    return jnp.dot(a, b, preferred_element_type=F32)


def _dot_nt(a, b):
    return lax.dot_general(a, b, (((1,), (1,)), ((), ())), preferred_element_type=F32)


def _split3(x):
    hi = x.astype(BF16)
    r1 = x - hi.astype(F32)
    mid = r1.astype(BF16)
    lo = (r1 - mid.astype(F32)).astype(BF16)
    return hi, mid, lo


def _dot_exact_lhs(a, x):
    hi, mid, lo = _split3(x)
    return _dot(a, hi) + _dot(a, mid) + _dot(a, lo)


def _ffn_kernel(h_ref, pre_ref, wg_ref, wu_ref, wd_ref, post_ref, o_ref, u_scr, acc_scr):
    j = pl.program_id(1)

    @pl.when(j == 0)
    def _():
        u_scr[...] = _rms(h_ref[...], pre_ref[...]).astype(BF16)
        acc_scr[...] = jnp.zeros_like(acc_scr)

    u = u_scr[...]
    g = _dot(u, wg_ref[...])
    v = _dot(u, wu_ref[...])
    acc_scr[...] += _dot((_silu(g) * v).astype(BF16), wd_ref[...])

    @pl.when(j == pl.num_programs(1) - 1)
    def _():
        o_ref[...] = h_ref[...] + HALF * _rms(acc_scr[...], post_ref[...])


def _ffn(h, pre_w, w_gu, w_down, post_w, layer):
    n, d = h.shape
    f = w_down.shape[1]
    nf = f // FF_TILE
    return pl.pallas_call(
        _ffn_kernel,
        grid=(n // TOKEN_TILE, nf),
        in_specs=[
            pl.BlockSpec((TOKEN_TILE, d), lambda i, j: (i, 0)),
            pl.BlockSpec((None, 1, d), lambda i, j: (layer, 0, 0)),
            pl.BlockSpec((None, d, FF_TILE), lambda i, j: (layer, 0, j)),
            pl.BlockSpec((None, d, FF_TILE), lambda i, j: (layer, 0, j + nf)),
            pl.BlockSpec((None, FF_TILE, d), lambda i, j: (layer, j, 0)),
            pl.BlockSpec((None, 1, d), lambda i, j: (layer, 0, 0)),
        ],
        out_specs=pl.BlockSpec((TOKEN_TILE, d), lambda i, j: (i, 0)),
        out_shape=jax.ShapeDtypeStruct((n, d), F32),
        scratch_shapes=[pltpu.VMEM((TOKEN_TILE, d), BF16), pltpu.VMEM((TOKEN_TILE, d), F32)],
        compiler_params=_cparams(("parallel", "arbitrary")),
        name="ffn",
    )(h, pre_w, w_gu, w_gu, w_down, post_w)


def _inproj_kernel(h_ref, nw_ref, w_ref, o_ref, u_scr):
    @pl.when(pl.program_id(1) == 0)
    def _():
        u_scr[...] = _rms(h_ref[...], nw_ref[...]).astype(BF16)

    o_ref[...] = _dot(u_scr[...], w_ref[...])


def _inproj(h, nw, w, layer, col_tile):
    n, d = h.shape
    cols = w.shape[2]
    return pl.pallas_call(
        _inproj_kernel,
        grid=(n // PROJ_TOKEN_TILE, cols // col_tile),
        in_specs=[
            pl.BlockSpec((PROJ_TOKEN_TILE, d), lambda i, j: (i, 0)),
            pl.BlockSpec((None, 1, d), lambda i, j: (layer, 0, 0)),
            pl.BlockSpec((None, d, col_tile), lambda i, j: (layer, 0, j)),
        ],
        out_specs=pl.BlockSpec((PROJ_TOKEN_TILE, col_tile), lambda i, j: (i, j)),
        out_shape=jax.ShapeDtypeStruct((n, cols), F32),
        scratch_shapes=[pltpu.VMEM((PROJ_TOKEN_TILE, d), BF16)],
        compiler_params=_cparams(("parallel", "arbitrary")),
        name="inproj",
    )(h, nw, w)


def _ssd_kernel(z_ref, xs_ref, bc_ref, dt_ref, cwx_ref, cbx_ref, cwb_ref, cbb_ref, dtb_ref,
                alog_ref, dsk_ref, nw_ref, e_ref, o_ref, xpad, bpad, hst, *, n_heads):
    L = SSD_CHUNK
    N = SSD_STATE
    P = SSD_HEAD_DIM
    G = SSD_GROUPS
    hg = n_heads // G
    gw = hg * P
    halo = SUBLANES

    @pl.when(pl.program_id(1) == 0)
    def _():
        xpad[0:halo, :] = jnp.zeros((halo, xpad.shape[1]), F32)
        bpad[0:halo, :] = jnp.zeros((halo, bpad.shape[1]), F32)
        hst[...] = jnp.zeros_like(hst)

    xpad[halo:halo + L, :] = xs_ref[...]
    bpad[halo:halo + L, :] = bc_ref[...]

    def conv_silu(pad, w_ref, b_ref):
        acc = b_ref[...]
        for k in range(SSD_CONV):
            off = halo - (SSD_CONV - 1) + k
            acc = acc + w_ref[k:k + 1, :] * pad[off:off + L, :]
        return _silu(acc)

    x = conv_silu(xpad, cwx_ref, cbx_ref)
    bcv = conv_silu(bpad, cwb_ref, cbb_ref)
    xpad[0:halo, :] = xpad[L:L + halo, :]
    bpad[0:halo, :] = bpad[L:L + halo, :]

    lane = lax.broadcasted_iota(jnp.int32, (1, LANES), 1)
    dtv = dt_ref[...] + dtb_ref[...]
    dt = jnp.maximum(dtv, 0.0) + jnp.log(1.0 + jnp.exp(-jnp.abs(dtv)))
    dt = jnp.where(lane < n_heads, dt, 0.0)
    a = -jnp.exp(alog_ref[...])
    da = dt * a
    row = lax.broadcasted_iota(jnp.int32, (L, L), 0)
    col = lax.broadcasted_iota(jnp.int32, (L, L), 1)
    tril = row >= col
    cs = _dot_exact_lhs(tril.astype(BF16), da)
    cs_t = cs.T
    dt_t = dt.T
    cs_last = cs[L - 1:L, :]
    e = e_ref[...]
    ecs = _dot(jnp.exp(cs).astype(BF16), e)
    dend = _dot((jnp.exp(cs_last - cs) * dt).astype(BF16), e)
    lane_p = lax.broadcasted_iota(jnp.int32, (L, 2 * P), 1)

    ys = []
    for g in range(G):
        bm = bcv[:, g * N:(g + 1) * N]
        cm = bcv[:, G * N + g * N:G * N + (g + 1) * N]
        bm16 = bm.astype(BF16)
        cm16 = cm.astype(BF16)
        cb = _dot_nt(cm16, bm16)
        xg = x[:, g * gw:(g + 1) * gw]
        yd = []
        for hp in range(hg // 2):
            ws = []
            for hh in range(2):
                h = g * hg + hp * 2 + hh
                seg = cs[:, h:h + 1] - cs_t[h:h + 1, :]
                dec = jnp.exp(jnp.where(tril, seg, -jnp.inf))
                ws.append(cb * dec * dt_t[h:h + 1, :])
            wcat = jnp.concatenate(ws, axis=1).astype(BF16)
            xp = xg[:, hp * 2 * P:(hp + 1) * 2 * P]
            xbd = jnp.concatenate([jnp.where(lane_p < P, xp, 0.0),
                                   jnp.where(lane_p >= P, xp, 0.0)], axis=0).astype(BF16)
            yd.append(_dot(wcat, xbd))
        y_diag = jnp.concatenate(yd, axis=1)
        hprev = hst[g]
        y_off = _dot(cm16, hprev.astype(BF16)) * ecs[:, g * gw:(g + 1) * gw]
        st = _dot(bm.T.astype(BF16), (xg * dend[:, g * gw:(g + 1) * gw]).astype(BF16))
        hst[g] = hprev * ecs[L - 1:L, g * gw:(g + 1) * gw] + st
        y = y_diag + y_off + dsk_ref[:, g * gw:(g + 1) * gw] * xg
        y = y * _silu(z_ref[:, g * gw:(g + 1) * gw])
        y = y * lax.rsqrt(jnp.mean(y * y, axis=-1, keepdims=True) + EPS)
        ys.append(y * nw_ref[:, g * gw:(g + 1) * gw])
    o_ref[...] = jnp.concatenate(ys, axis=1).astype(o_ref.dtype)


def _ssd(proj3, cols, cwx, cbx, cwb, cbb, dtb, alog, dsk, nw, e, layer, n_heads):
    b, s, _ = proj3.shape
    width = n_heads * SSD_HEAD_DIM
    bcw = 2 * SSD_GROUPS * SSD_STATE
    L = SSD_CHUNK
    gw = width // SSD_GROUPS

    def wspec(c):
        return pl.BlockSpec((None, 1, c), lambda bi, ci: (layer, 0, 0))

    return pl.pallas_call(
        functools.partial(_ssd_kernel, n_heads=n_heads),
        grid=(b, s // L),
        in_specs=[
            pl.BlockSpec((None, L, width), lambda bi, ci: (bi, ci, cols["z"] // width)),
            pl.BlockSpec((None, L, width), lambda bi, ci: (bi, ci, cols["xs"] // width)),
            pl.BlockSpec((None, L, bcw), lambda bi, ci: (bi, ci, cols["bc"] // bcw)),
            pl.BlockSpec((None, L, LANES), lambda bi, ci: (bi, ci, cols["dt"] // LANES)),
            pl.BlockSpec((None, SSD_CONV, width), lambda bi, ci: (layer, 0, 0)),
            wspec(width),
            pl.BlockSpec((None, SSD_CONV, bcw), lambda bi, ci: (layer, 0, 0)),
            wspec(bcw),
            wspec(LANES),
            wspec(LANES),
            wspec(width),
            wspec(width),
            pl.BlockSpec((LANES, width), lambda bi, ci: (0, 0)),
        ],
        out_specs=pl.BlockSpec((None, L, width), lambda bi, ci: (bi, ci, 0)),
        out_shape=jax.ShapeDtypeStruct((b, s, width), BF16),
        scratch_shapes=[pltpu.VMEM((L + SUBLANES, width), F32),
                        pltpu.VMEM((L + SUBLANES, bcw), F32),
                        pltpu.VMEM((SSD_GROUPS, SSD_STATE, gw), F32)],
        compiler_params=_cparams(("parallel", "arbitrary")),
        name="ssd",
    )(proj3, proj3, proj3, proj3, cwx, cbx, cwb, cbb, dtb, alog, dsk, nw, e)


def _cnv_kernel(glu_ref, w_ref, b_ref, lw_ref, lb_ref, o_ref, pad, shifted):
    T = CONV_TILE
    c = pad.shape[1]

    @pl.when(pl.program_id(1) == 0)
    def _():
        pad[0:CONV_HALO, :] = jnp.zeros((CONV_HALO, c), F32)

    pad[CONV_HALO:CONV_HALO + T, :] = glu_ref[:, 0:c] * jax.nn.sigmoid(glu_ref[:, c:2 * c])
    span = T + CONV_HALO - SUBLANES
    for r in range(1, SUBLANES):
        shifted[r - 1] = pad[r:r + span, :]
    acc = b_ref[...]
    for k in range(CONV_KERNEL):
        off = CONV_HALO - (CONV_KERNEL - 1) + k
        r, base = off % SUBLANES, off - off % SUBLANES
        src = pad[base:base + T, :] if r == 0 else shifted[r - 1, base:base + T, :]
        acc = acc + w_ref[k:k + 1, :] * src
    pad[0:CONV_HALO, :] = pad[T:T + CONV_HALO, :]
    mu = jnp.mean(acc, axis=-1, keepdims=True)
    cen = acc - mu
    var = jnp.mean(cen * cen, axis=-1, keepdims=True)
    v = cen * lax.rsqrt(var + EPS) * lw_ref[...] + lb_ref[...]
    o_ref[...] = _silu(v).astype(o_ref.dtype)


def _cnv(proj3, col, w, bias, lw, lb, layer):
    b, s, _ = proj3.shape
    c = w.shape[2]
    T = CONV_TILE
    return pl.pallas_call(
        _cnv_kernel,
        grid=(b, s // T),
        in_specs=[
            pl.BlockSpec((None, T, 2 * c), lambda bi, ti: (bi, ti, col // (2 * c))),
            pl.BlockSpec((None, CONV_KERNEL, c), lambda bi, ti: (layer, 0, 0)),
            pl.BlockSpec((None, 1, c), lambda bi, ti: (layer, 0, 0)),
            pl.BlockSpec((None, 1, c), lambda bi, ti: (layer, 0, 0)),
            pl.BlockSpec((None, 1, c), lambda bi, ti: (layer, 0, 0)),
        ],
        out_specs=pl.BlockSpec((None, T, c), lambda bi, ti: (bi, ti, 0)),
        out_shape=jax.ShapeDtypeStruct((b, s, c), BF16),
        scratch_shapes=[pltpu.VMEM((T + CONV_HALO, c), F32),
                        pltpu.VMEM((SUBLANES - 1, T + CONV_HALO - SUBLANES, c), F32)],
        compiler_params=_cparams(("parallel", "arbitrary")),
        name="cnv",
    )(proj3, w, bias, lw, lb)


def _cmp_kernel(u_ref, pe_ref, w1_ref, b1_ref, w2_ref, w2t_ref, on_ref, ot_ref):
    u = u_ref[...]
    half = u.shape[1]
    nc = u.shape[0]
    a0 = (u + pe_ref[:, 0:half]).astype(BF16)
    a1 = (u + pe_ref[:, half:2 * half]).astype(BF16)
    h0 = _dot(a0, w1_ref[0:half, :])
    h1 = _dot(a1, w1_ref[half:2 * half, :])
    pre = h0 + pltpu.roll(h1, nc - 1, 0) + b1_ref[...]
    act = _silu(pre).astype(BF16)
    on_ref[...] = _dot(act, w2_ref[...]).astype(on_ref.dtype)
    ot_ref[...] = _dot_nt(w2t_ref[...], act).astype(ot_ref.dtype)


def _cmp(u16, pe, w1, b1, w2, w2t, layer):
    _, b, g, nc, uw = u16.shape
    hid = w1.shape[3]
    dh = w2.shape[3]
    return pl.pallas_call(
        _cmp_kernel,
        grid=(2, b, g),
        in_specs=[
            pl.BlockSpec((None, None, None, nc, uw), lambda k, bi, gi: (k, bi, gi, 0, 0)),
            pl.BlockSpec((None, None, 1, 2 * uw), lambda k, bi, gi: (k, layer, 0, 0)),
            pl.BlockSpec((None, None, 2 * uw, hid), lambda k, bi, gi: (k, layer, 0, 0)),
            pl.BlockSpec((None, None, 1, hid), lambda k, bi, gi: (k, layer, 0, 0)),
            pl.BlockSpec((None, None, hid, dh), lambda k, bi, gi: (k, layer, 0, 0)),
            pl.BlockSpec((None, None, dh, hid), lambda k, bi, gi: (k, layer, 0, 0)),
        ],
        out_specs=[
            pl.BlockSpec((None, None, None, nc, dh), lambda k, bi, gi: (k, bi, gi, 0, 0)),
            pl.BlockSpec((None, None, None, dh, nc), lambda k, bi, gi: (k, bi, gi, 0, 0)),
        ],
        out_shape=[jax.ShapeDtypeStruct((2, b, g, nc, dh), BF16),
                   jax.ShapeDtypeStruct((2, b, g, dh, nc), BF16)],
        compiler_params=_cparams(("parallel", "parallel", "parallel")),
        name="cmp",
    )(u16, pe, w1, b1, w2, w2t)


def _nsa_kernel(q_ref, gl_ref, kc_ref, vct_ref, ks_ref, vst_ref, ksd_ref, vsd_ref, kwd_ref,
                vwd_ref, ovt_ref, tb_ref, o_ref, s_scr, *, n_sel, nqb, gate_lane):
    Q = Q_BLOCK
    dh = NSA_HEAD_DIM
    hpg = q_ref.shape[1] // dh
    ncol = hpg * Q
    nc = kc_ref.shape[0]
    nbp = ovt_ref.shape[0]
    kt = ks_ref.shape[1]
    qi = pl.program_id(2)
    q0 = qi * Q

    qt = q_ref[...] * (dh ** -0.5 * math.log2(math.e))
    q = jnp.concatenate([qt[:, h * dh:(h + 1) * dh] for h in range(hpg)], axis=0).astype(BF16)
    gl_t = gl_ref[...].T
    gates = [jax.nn.sigmoid(jnp.concatenate(
        [gl_t[gate_lane + 3 * h + k:gate_lane + 3 * h + k + 1, :] for h in range(hpg)], axis=1))
        for k in range(3)]
    lane = lax.broadcasted_iota(jnp.int32, (1, ncol), 1)
    tok = q0 + (lane & (Q - 1))

    def tile_cols(a):
        return jnp.concatenate([a] * hpg, axis=1)

    r_i = lax.broadcasted_iota(jnp.int32, (Q, Q), 0)
    c_i = lax.broadcasted_iota(jnp.int32, (Q, Q), 1)
    tri_le = jnp.where(r_i <= c_i, 0.0, NEG)
    tri_gt = jnp.where(r_i > c_i, 0.0, NEG)

    off = pl.multiple_of((nqb - 1 - qi) * (Q // CMP_STRIDE), SUBLANES)
    bias_c = tb_ref[pl.ds(off, nc), :]
    scm = _dot_nt(kc_ref[...], q) + tile_cols(bias_c)
    m_c = jnp.max(scm, axis=0, keepdims=True)
    p_c = jnp.exp2(scm - m_c)
    l_c = jnp.sum(p_c, axis=0, keepdims=True)
    p_c = p_c * jnp.where(tok >= CMP_BLOCK - 1, 1.0 / l_c, 0.0)
    o_c = _dot(vct_ref[...], p_c.astype(BF16))

    psum = p_c[:, 0:Q]
    for h in range(1, hpg):
        psum = psum + p_c[:, h * Q:(h + 1) * Q]
    imp = _dot_exact_lhs(ovt_ref[...], psum)

    nch = WINDOW // Q + 1
    tri_d = tile_cols(tri_le)
    first = qi - (nch - 1)
    idx = [jnp.maximum(first + c, 0) for c in range(nch)]
    before = [jnp.where(first + c >= 0, 0.0, NEG) for c in range(nch)]
    sw = _dot_nt(jnp.concatenate([kwd_ref[i] for i in idx], axis=0), q)
    sw = jnp.concatenate(
        [sw[0:Q] + (tile_cols(tri_gt) + before[0])]
        + [sw[c * Q:(c + 1) * Q] + before[c] for c in range(1, nch - 1)]
        + [sw[(nch - 1) * Q:nch * Q] + tri_d], axis=0)
    m_w = jnp.max(sw, axis=0, keepdims=True)
    p_w = jnp.exp2(sw - m_w).astype(BF16)
    acc_w = jnp.zeros((V_ROWS, ncol), F32)
    for c in range(nch):
        acc_w = acc_w + _dot(vwd_ref[idx[c]], p_w[c * Q:(c + 1) * Q, :])
    o_w = acc_w[0:dh, :] * (1.0 / acc_w[dh:dh + 1, :])
    o_cw = gates[0] * o_c + gates[2] * o_w

    s_d = _dot_nt(ksd_ref[...], q) + tri_d
    m_d = jnp.max(s_d, axis=0, keepdims=True)

    blk = lax.broadcasted_iota(jnp.int32, (nbp, Q), 0)
    t1 = q0 + lax.broadcasted_iota(jnp.int32, (nbp, Q), 1)
    cur = lax.shift_right_logical(t1, SLC_BLOCK.bit_length() - 1)
    forced = (blk == 0) | (blk == cur) | (blk == cur - 1)
    val = jnp.where(blk * SLC_BLOCK <= t1, jnp.where(forced, imp + FORCE_BONUS, imp), NEG)
    blk_f = blk.astype(F32)
    for _ in range(n_sel):
        mx = jnp.max(val, axis=0, keepdims=True)
        idx = jnp.min(jnp.where(val == mx, blk_f, float(nbp)), axis=0, keepdims=True)
        val = jnp.where(blk_f == idx, -jnp.inf, val)
    bias = jnp.where((val == -jnp.inf) & (blk < qi * (Q // SLC_BLOCK)), 0.0, NEG)
    bias_t = tile_cols(bias).T.astype(BF16)
    qa = jnp.concatenate([bias_t, q], axis=1)

    n_trip = q0 // kt + 1

    def score_trip(g, m):
        s = _dot_nt(ks_ref[g], qa)
        s_scr[g] = s
        return jnp.maximum(m, jnp.max(s, axis=0, keepdims=True))

    def paired(trip, carry):
        def quad(i, c):
            for u in range(4):
                c = trip(4 * i + u, c)
            return c

        carry = lax.fori_loop(0, n_trip // 4, quad, carry)
        rem = n_trip % 4
        base = n_trip - rem
        carry = lax.cond(rem >= 2, lambda c: trip(base + 1, trip(base, c)), lambda c: c, carry)
        return lax.cond(rem % 2 == 1, lambda c: trip(n_trip - 1, c), lambda c: c, carry)

    m_s = paired(score_trip, m_d)

    def value_trip(g, acc):
        p = jnp.exp2(s_scr[g] - m_s).astype(BF16)
        return acc + _dot(vst_ref[g], p)

    acc_s = _dot(vsd_ref[...], jnp.exp2(s_d - m_s).astype(BF16))
    acc_s = paired(value_trip, acc_s)
    o_s = acc_s[0:dh, :] * (1.0 / acc_s[dh:dh + 1, :])

    o_t = o_cw + gates[1] * o_s
    per = LANES // dh
    outs = []
    for hp in range(hpg // per):
        stack = jnp.concatenate([o_t[:, (hp * per + k) * Q:(hp * per + k + 1) * Q]
                                 for k in range(per)], axis=0)
        outs.append(stack.T)
    o_ref[...] = jnp.concatenate(outs, axis=1).astype(o_ref.dtype)


def _nsa(proj3, q_col, gate_col, gate_lane, hpg, cn, ct, ks, vst, ksd, vsd, kwd, vwd, ovt, tb, n_sel):
    b, s, _ = proj3.shape
    g, nqb, dh = ks.shape[1], ksd.shape[2], NSA_HEAD_DIM
    nc = cn.shape[3]
    nt, kt, ka = ks.shape[2], ks.shape[3], ks.shape[4]
    nbp = ovt.shape[0]
    ncol = hpg * Q_BLOCK
    qw = hpg * dh
    vr = vst.shape[3]
    once = pl.Buffered(1)
    return pl.pallas_call(
        functools.partial(_nsa_kernel, n_sel=n_sel, nqb=nqb, gate_lane=gate_lane),
        grid=(b, g, nqb),
        in_specs=[
            pl.BlockSpec((None, Q_BLOCK, qw), lambda bi, gi, qi: (bi, qi, q_col // qw + gi)),
            pl.BlockSpec((None, Q_BLOCK, LANES), lambda bi, gi, qi: (bi, qi, gate_col // LANES + gi)),
            pl.BlockSpec((None, None, None, nc, dh), lambda bi, gi, qi: (0, bi, gi, 0, 0)),
            pl.BlockSpec((None, None, None, dh, nc), lambda bi, gi, qi: (1, bi, gi, 0, 0)),
            pl.BlockSpec((None, None, nt, kt, ka), lambda bi, gi, qi: (bi, gi, 0, 0, 0),
                         pipeline_mode=once),
            pl.BlockSpec((None, None, nt, vr, kt), lambda bi, gi, qi: (bi, gi, 0, 0, 0),
                         pipeline_mode=once),
            pl.BlockSpec((None, None, None, Q_BLOCK, dh), lambda bi, gi, qi: (bi, gi, qi, 0, 0)),
            pl.BlockSpec((None, None, None, vr, Q_BLOCK), lambda bi, gi, qi: (bi, gi, qi, 0, 0)),
            pl.BlockSpec((None, None, nqb, Q_BLOCK, dh), lambda bi, gi, qi: (bi, gi, 0, 0, 0),
                         pipeline_mode=once),
            pl.BlockSpec((None, None, nqb, vr, Q_BLOCK), lambda bi, gi, qi: (bi, gi, 0, 0, 0),
                         pipeline_mode=once),
            pl.BlockSpec((nbp, nc), lambda bi, gi, qi: (0, 0)),
            pl.BlockSpec(tb.shape, lambda bi, gi, qi: (0, 0)),
        ],
        out_specs=pl.BlockSpec((None, Q_BLOCK, qw), lambda bi, gi, qi: (bi, qi, gi)),
        out_shape=jax.ShapeDtypeStruct((b, s, g * qw), BF16),
        scratch_shapes=[pltpu.VMEM((nt, kt, ncol), F32)],
        compiler_params=_cparams(("parallel", "parallel", "arbitrary")),
        name="nsa",
    )(proj3, proj3, cn, ct, ks, vst, ksd, vsd, kwd, vwd, ovt, tb)


def _kvprep_kernel(pc_ref, ps_ref, pw_ref, craw_ref, ks_ref, vst_ref, ksd_ref, vsd_ref,
                   kwd_ref, vwd_ref):
    dh = NSA_HEAD_DIM
    Q = Q_BLOCK
    kt = ps_ref.shape[0]
    nbp = ks_ref.shape[1] - dh
    j = pl.program_id(2)
    pc = pc_ref[...]
    craw_ref[0] = pc[:, 0:dh]
    craw_ref[1] = pc[:, dh:2 * dh]

    tail_row = lax.broadcasted_iota(jnp.int32, (V_ROWS - dh, kt), 0)
    tail = jnp.where(tail_row == 0, 1.0, 0.0).astype(BF16)

    def k_and_vt(p_ref):
        p = p_ref[...]
        k = p[:, 0:dh].astype(BF16)
        vt = jnp.concatenate([p.T[dh:2 * dh, :].astype(BF16), tail], axis=0)
        return k, vt

    k_s, vt_s = k_and_vt(ps_ref)
    key = lax.broadcasted_iota(jnp.int32, (kt, nbp), 0)
    bcol = lax.broadcasted_iota(jnp.int32, (kt, nbp), 1)
    blk_of_key = j * (kt // SLC_BLOCK) + lax.shift_right_logical(key, SLC_BLOCK.bit_length() - 1)
    ks_ref[:, 0:nbp] = jnp.where(bcol == blk_of_key, 1.0, 0.0).astype(BF16)
    ks_ref[:, nbp:nbp + dh] = k_s
    vst_ref[...] = vt_s
    k_w, vt_w = k_and_vt(pw_ref)
    for c in range(kt // Q):
        ksd_ref[c] = k_s[c * Q:(c + 1) * Q, :]
        vsd_ref[c] = vt_s[:, c * Q:(c + 1) * Q]
        kwd_ref[c] = k_w[c * Q:(c + 1) * Q, :]
        vwd_ref[c] = vt_w[:, c * Q:(c + 1) * Q]


def _kvprep(proj3, kv_col, g, kt, nbp):
    b, s, _ = proj3.shape
    dh = NSA_HEAD_DIM
    nt, nqb, cpt = s // kt, s // Q_BLOCK, kt // Q_BLOCK
    pair = 2 * dh
    base = kv_col // pair

    def pspec(branch):
        return pl.BlockSpec((None, kt, pair), lambda bi, gi, j: (bi, j, base + 3 * gi + branch))

    chunk_k = pl.BlockSpec((None, None, cpt, Q_BLOCK, dh), lambda bi, gi, j: (bi, gi, j, 0, 0))
    chunk_v = pl.BlockSpec((None, None, cpt, V_ROWS, Q_BLOCK), lambda bi, gi, j: (bi, gi, j, 0, 0))
    return pl.pallas_call(
        _kvprep_kernel,
        grid=(b, g, nt),
        in_specs=[pspec(0), pspec(1), pspec(2)],
        out_specs=[
            pl.BlockSpec((2, None, None, kt, dh), lambda bi, gi, j: (0, bi, gi, j, 0)),
            pl.BlockSpec((None, None, None, kt, nbp + dh), lambda bi, gi, j: (bi, gi, j, 0, 0)),
            pl.BlockSpec((None, None, None, V_ROWS, kt), lambda bi, gi, j: (bi, gi, j, 0, 0)),
            chunk_k, chunk_v, chunk_k, chunk_v,
        ],
        out_shape=[
            jax.ShapeDtypeStruct((2, b, g, s, dh), F32),
            jax.ShapeDtypeStruct((b, g, nt, kt, nbp + dh), BF16),
            jax.ShapeDtypeStruct((b, g, nt, V_ROWS, kt), BF16),
            jax.ShapeDtypeStruct((b, g, nqb, Q_BLOCK, dh), BF16),
            jax.ShapeDtypeStruct((b, g, nqb, V_ROWS, Q_BLOCK), BF16),
            jax.ShapeDtypeStruct((b, g, nqb, Q_BLOCK, dh), BF16),
            jax.ShapeDtypeStruct((b, g, nqb, V_ROWS, Q_BLOCK), BF16),
        ],
        compiler_params=_cparams(("parallel", "parallel", "parallel")),
        name="kvprep",
    )(proj3, proj3, proj3)


def _outproj_kernel(h_ref, ys_ref, yn_ref, yc_ref, w_ref, nw_ref, o_ref):
    w0 = ys_ref.shape[1]
    w1 = w0 + yn_ref.shape[1]
    w2 = w1 + yc_ref.shape[1]
    m = (_dot(ys_ref[...], w_ref[0:w0, :]) + _dot(yn_ref[...], w_ref[w0:w1, :])
         + _dot(yc_ref[...], w_ref[w1:w2, :]))
    o_ref[...] = h_ref[...] + _rms(m, nw_ref[...])


def _outproj(h, ys, yn, yc, w, nw, layer):
    n, d = h.shape
    dm = w.shape[1]
    return pl.pallas_call(
        _outproj_kernel,
        grid=(n // TOKEN_TILE,),
        in_specs=[
            pl.BlockSpec((TOKEN_TILE, d), lambda i: (i, 0)),
            pl.BlockSpec((TOKEN_TILE, ys.shape[1]), lambda i: (i, 0)),
            pl.BlockSpec((TOKEN_TILE, yn.shape[1]), lambda i: (i, 0)),
            pl.BlockSpec((TOKEN_TILE, yc.shape[1]), lambda i: (i, 0)),
            pl.BlockSpec((None, dm, d), lambda i: (layer, 0, 0)),
            pl.BlockSpec((None, 1, d), lambda i: (layer, 0, 0)),
        ],
        out_specs=pl.BlockSpec((TOKEN_TILE, d), lambda i: (i, 0)),
        out_shape=jax.ShapeDtypeStruct((n, d), F32),
        compiler_params=_cparams(("parallel",)),
        name="outproj",
    )(h, ys, yn, yc, w, nw)


def _nsa_constants(nc, nb, nbp, nqb):
    c0 = np.arange(nc)[None, :] * CMP_STRIDE
    s0 = np.arange(nbp)[:, None] * SLC_BLOCK
    ov = np.maximum(np.minimum(c0 + CMP_BLOCK, s0 + SLC_BLOCK) - np.maximum(c0, s0), 0) / CMP_STRIDE
    ov[nb:] = 0.0
    upq = Q_BLOCK // CMP_STRIDE
    r = np.arange(nc + upq * (nqb - 1))[:, None]
    tl = np.arange(Q_BLOCK)[None, :]
    tb = np.where(CMP_STRIDE * r + (CMP_BLOCK - 1) - Q_BLOCK * (nqb - 1) <= tl, 0.0, NEG)
    return jnp.asarray(ov, BF16), jnp.asarray(tb, F32)


def kernel(x, ffn1_pre_norm, ffn1_w_gu, ffn1_w_down, ffn1_post_norm, mix_pre_norm, w_in,
           ssd_conv_w, ssd_conv_b, ssd_dt_bias, ssd_a_log, ssd_d, ssd_norm_w,
           nsa_k_pe, nsa_k_w1, nsa_k_b1, nsa_k_w2, nsa_v_pe, nsa_v_w1, nsa_v_b1, nsa_v_w2,
           cnv_dw_w, cnv_dw_b, cnv_ln_w, cnv_ln_b, w_out, mix_post_norm,
           ffn2_pre_norm, ffn2_w_gu, ffn2_w_down, ffn2_post_norm):
    bsz, s, d = x.shape
    depth = w_in.shape[0]
    n = bsz * s
    n_heads = ssd_dt_bias.shape[1]
    ssd_w = n_heads * SSD_HEAD_DIM
    bcw = 2 * SSD_GROUPS * SSD_STATE
    cch = cnv_dw_w.shape[2]
    d_in = w_in.shape[2]
    kvw = NSA_KV_GROUPS * NSA_HEAD_DIM
    n_gate = d_in - (2 * ssd_w + bcw + n_heads + 6 * kvw + 2 * cch)
    nsa_heads = n_gate // (NSA_HEAD_DIM + 3)
    nsa_w = nsa_heads * NSA_HEAD_DIM
    G = NSA_KV_GROUPS
    hpg = nsa_heads // G
    dh = NSA_HEAD_DIM

    o_z, o_xbc, o_dt = 0, ssd_w, ssd_w + ssd_w + bcw
    o_q = o_dt + n_heads
    o_kv = o_q + nsa_w
    o_gl = o_kv + 6 * kvw
    o_glu = o_gl + 3 * nsa_heads
    gl_w = 3 * hpg
    src = [("z", o_z, ssd_w), ("glu", o_glu, 2 * cch), ("xs", o_xbc, ssd_w),
           ("bc", o_xbc + ssd_w, bcw), ("q", o_q, nsa_w)]
    for gi in range(G):
        for br in range(6):
            src.append(("kv" if gi == 0 and br == 0 else None, o_kv + br * kvw + gi * dh, dh))
    cols, pos, pieces = {}, 0, []
    for name, o, w in src:
        if name:
            cols[name] = pos
        pieces.append(w_in[:, :, o:o + w].astype(BF16))
        pos += w
    zeros = lambda w: jnp.zeros((depth, d, w), BF16)
    cols["dt"] = cols["gate"] = pos
    assert pos % LANES == 0 and n_heads + gl_w <= LANES
    for gi in range(G):
        head = w_in[:, :, o_dt:o_dt + n_heads].astype(BF16) if gi == 0 else zeros(n_heads)
        pieces += [head, w_in[:, :, o_gl + gi * gl_w:o_gl + (gi + 1) * gl_w].astype(BF16),
                   zeros(LANES - n_heads - gl_w)]
        pos += LANES
    col_tile = 10 * LANES
    total = -(-pos // col_tile) * col_tile
    if total > pos:
        pieces.append(zeros(total - pos))
    w_in_r = jnp.concatenate(pieces, axis=2)
    assert cols["z"] % ssd_w == 0 and cols["xs"] % ssd_w == 0 and cols["bc"] % bcw == 0
    assert cols["glu"] % (2 * cch) == 0 and cols["q"] % (hpg * dh) == 0 and cols["kv"] % (2 * dh) == 0

    r3 = lambda a: a.reshape(depth, 1, -1)
    w1_gu, w1_dn = ffn1_w_gu.astype(BF16), ffn1_w_down.astype(BF16)
    w2_gu, w2_dn = ffn2_w_gu.astype(BF16), ffn2_w_down.astype(BF16)
    w_out16 = w_out.astype(BF16)
    cwx, cwb = ssd_conv_w[:, :, :ssd_w], ssd_conv_w[:, :, ssd_w:]
    cbx, cbb = r3(ssd_conv_b[:, :ssd_w]), r3(ssd_conv_b[:, ssd_w:])
    padl = lambda a: jnp.pad(a, ((0, 0), (0, LANES - a.shape[1]))).reshape(depth, 1, LANES)
    dtb, alog = padl(ssd_dt_bias), padl(ssd_a_log)
    dsk = r3(jnp.repeat(ssd_d, SSD_HEAD_DIM, axis=1))
    e_np = np.zeros((LANES, ssd_w), np.float32)
    for hh in range(n_heads):
        e_np[hh, hh * SSD_HEAD_DIM:(hh + 1) * SSD_HEAD_DIM] = 1.0
    e_mat = jnp.asarray(e_np, dtype=BF16)
    pe = jnp.stack([nsa_k_pe, nsa_v_pe]).reshape(2, depth, 1, CMP_BLOCK * dh)
    cw1 = jnp.stack([nsa_k_w1, nsa_v_w1]).astype(BF16)
    cb1 = jnp.stack([nsa_k_b1, nsa_v_b1]).reshape(2, depth, 1, -1)
    cw2 = jnp.stack([nsa_k_w2, nsa_v_w2]).astype(BF16)
    cw2t = jnp.swapaxes(cw2, 2, 3)
    nc = s // CMP_STRIDE
    nb = s // SLC_BLOCK
    nbp = -(-nb // LANES) * LANES
    nqb = s // Q_BLOCK
    kt = min(KV_TILE, s)
    nt = s // kt
    n_sel = min(SLC_TOPN, nb)
    ovt, tb = _nsa_constants(nc, nb, nbp, nqb)

    def nsa_branch(proj3, l):
        craw, ks, vst, ksd, vsd, kwd, vwd = _kvprep(proj3, cols["kv"], G, kt, nbp)
        u16 = craw.reshape(2, bsz, G, nc, CMP_STRIDE * dh)
        cn, ct = _cmp(u16, pe, cw1, cb1, cw2, cw2t, l)
        return _nsa(proj3, cols["q"], cols["gate"], n_heads, hpg, cn, ct, ks, vst, ksd, vsd,
                    kwd, vwd, ovt, tb, n_sel)

    h = x.reshape(n, d)
    for l in range(depth):
        h = _ffn(h, r3(ffn1_pre_norm), w1_gu, w1_dn, r3(ffn1_post_norm), l)

        proj = _inproj(h, r3(mix_pre_norm), w_in_r, l, col_tile)
        proj3 = proj.reshape(bsz, s, total)
        y_ssd = _ssd(proj3, cols, cwx, cbx, cwb, cbb, dtb, alog, dsk, r3(ssd_norm_w), e_mat, l, n_heads)
        y_cnv = _cnv(proj3, cols["glu"], cnv_dw_w, r3(cnv_dw_b), r3(cnv_ln_w), r3(cnv_ln_b), l)
        y_nsa = nsa_branch(proj3, l)

        h = _outproj(h, y_ssd.reshape(n, ssd_w), y_nsa.reshape(n, nsa_w), y_cnv.reshape(n, cch),
                     w_out16, r3(mix_post_norm), l)
        h = _ffn(h, r3(ffn2_pre_norm), w2_gu, w2_dn, r3(ffn2_post_norm), l)
    return h.reshape(bsz, s, d)
```

```python
import functools
import math

import numpy as np
import jax
import jax.numpy as jnp
from jax import lax
from jax.experimental import pallas as pl
from jax.experimental.pallas import tpu as pltpu

F32 = jnp.float32
BF16 = jnp.bfloat16

SSD_HEAD_DIM = 64
SSD_GROUPS = 2
SSD_STATE = 128
SSD_CONV = 4
SSD_CHUNK = 128
NSA_HEAD_DIM = 64
NSA_KV_GROUPS = 2
CMP_BLOCK = 32
CMP_STRIDE = 16
SLC_BLOCK = 64
SLC_TOPN = 16
WINDOW = 512
Q_BLOCK = 128
FORCE_BONUS = 1000.0
CONV_KERNEL = 31
HALF = 0.5
EPS = 1e-6
NEG = -1e30

LANES = 128
SUBLANES = 8
BF16_ROWS = 16
VMEM_LIMIT = 56 * 1024 * 1024
TOKEN_TILE = 512
PROJ_TOKEN_TILE = 1024
FF_TILE = 512
KV_TILE = 1024
CONV_TILE = 256
CONV_HALO = 32
V_ROWS = NSA_HEAD_DIM + BF16_ROWS


def _cparams(sem):
    return pltpu.CompilerParams(dimension_semantics=sem, vmem_limit_bytes=VMEM_LIMIT)


def _rms(x, w):
    return (x * lax.rsqrt(jnp.mean(x * x, axis=-1, keepdims=True) + EPS)) * w


def _silu(x):
    return x * jax.nn.sigmoid(x)


def _dot(a, b):
    return jnp.dot(a, b, preferred_element_type=F32)


def _dot_nt(a, b):
    return lax.dot_general(a, b, (((1,), (1,)), ((), ())), preferred_element_type=F32)


def _split3(x):
    hi = x.astype(BF16)
    r1 = x - hi.astype(F32)
    mid = r1.astype(BF16)
    lo = (r1 - mid.astype(F32)).astype(BF16)
    return hi, mid, lo


def _dot_exact_lhs(a, x):
    hi, mid, lo = _split3(x)
    return _dot(a, hi) + _dot(a, mid) + _dot(a, lo)


def _ffn_kernel(h_ref, pre_ref, wg_ref, wu_ref, wd_ref, post_ref, o_ref, u_scr, acc_scr, *, nf):
    j = pl.program_id(1)
    last = nf - 1

    def step(first, final):
        if first:
            u = _rms(h_ref[...], pre_ref[...]).astype(BF16)
            u_scr[...] = u
        else:
            u = u_scr[...]
        g = _dot(u, wg_ref[...])
        v = _dot(u, wu_ref[...])
        part = _dot((_silu(g) * v).astype(BF16), wd_ref[...])
        acc = part if first else acc_scr[...] + part
        if final:
            o_ref[...] = h_ref[...] + HALF * _rms(acc, post_ref[...])
        else:
            acc_scr[...] = acc

    if nf == 1:
        step(True, True)
    else:
        pl.when(j == 0)(lambda: step(True, False))
        pl.when((j > 0) & (j < last))(lambda: step(False, False))
        pl.when(j == last)(lambda: step(False, True))


def _ffn(h, pre_w, w_gu, w_down, post_w, layer):
    n, d = h.shape
    f = w_down.shape[1]
    nf = f // FF_TILE
    return pl.pallas_call(
        functools.partial(_ffn_kernel, nf=nf),
        grid=(n // TOKEN_TILE, nf),
        in_specs=[
            pl.BlockSpec((TOKEN_TILE, d), lambda i, j: (i, 0)),
            pl.BlockSpec((None, 1, d), lambda i, j: (layer, 0, 0)),
            pl.BlockSpec((None, d, FF_TILE), lambda i, j: (layer, 0, j)),
            pl.BlockSpec((None, d, FF_TILE), lambda i, j: (layer, 0, j + nf)),
            pl.BlockSpec((None, FF_TILE, d), lambda i, j: (layer, j, 0)),
            pl.BlockSpec((None, 1, d), lambda i, j: (layer, 0, 0)),
        ],
        out_specs=pl.BlockSpec((TOKEN_TILE, d), lambda i, j: (i, 0)),
        out_shape=jax.ShapeDtypeStruct((n, d), F32),
        scratch_shapes=[pltpu.VMEM((TOKEN_TILE, d), BF16), pltpu.VMEM((TOKEN_TILE, d), F32)],
        compiler_params=_cparams(("parallel", "arbitrary")),
        name="ffn",
    )(h, pre_w, w_gu, w_gu, w_down, post_w)


def _inproj_kernel(h_ref, nw_ref, w_ref, o_ref, u_scr):
    @pl.when(pl.program_id(1) == 0)
    def _():
        u = _rms(h_ref[...], nw_ref[...]).astype(BF16)
        u_scr[...] = u
        o_ref[...] = _dot(u, w_ref[...])

    @pl.when(pl.program_id(1) > 0)
    def _():
        o_ref[...] = _dot(u_scr[...], w_ref[...])


def _inproj(h, nw, w, layer, col_tile):
    n, d = h.shape
    cols = w.shape[2]
    return pl.pallas_call(
        _inproj_kernel,
        grid=(n // PROJ_TOKEN_TILE, cols // col_tile),
        in_specs=[
            pl.BlockSpec((PROJ_TOKEN_TILE, d), lambda i, j: (i, 0)),
            pl.BlockSpec((None, 1, d), lambda i, j: (layer, 0, 0)),
            pl.BlockSpec((None, d, col_tile), lambda i, j: (layer, 0, j)),
        ],
        out_specs=pl.BlockSpec((PROJ_TOKEN_TILE, col_tile), lambda i, j: (i, j)),
        out_shape=jax.ShapeDtypeStruct((n, cols), F32),
        scratch_shapes=[pltpu.VMEM((PROJ_TOKEN_TILE, d), BF16)],
        compiler_params=_cparams(("parallel", "arbitrary")),
        name="inproj",
    )(h, nw, w)


def _ssd_kernel(z_ref, xs_ref, bc_ref, dt_ref, cwx_ref, cbx_ref, cwb_ref, cbb_ref, dtb_ref,
                alog_ref, dsk_ref, nw_ref, e_ref, o_ref, xpad, bpad, hst, *, n_heads):
    L = SSD_CHUNK
    N = SSD_STATE
    P = SSD_HEAD_DIM
    G = SSD_GROUPS
    hg = n_heads // G
    gw = hg * P
    halo = SUBLANES

    @pl.when(pl.program_id(1) == 0)
    def _():
        xpad[0:halo, :] = jnp.zeros((halo, xpad.shape[1]), F32)
        bpad[0:halo, :] = jnp.zeros((halo, bpad.shape[1]), F32)
        hst[...] = jnp.zeros_like(hst)

    xpad[halo:halo + L, :] = xs_ref[...]
    bpad[halo:halo + L, :] = bc_ref[...]

    def conv_silu(pad, w_ref, b_ref):
        acc = b_ref[...]
        for k in range(SSD_CONV):
            off = halo - (SSD_CONV - 1) + k
            acc = acc + w_ref[k:k + 1, :] * pad[off:off + L, :]
        return _silu(acc)

    x = conv_silu(xpad, cwx_ref, cbx_ref)
    bcv = conv_silu(bpad, cwb_ref, cbb_ref)
    xpad[0:halo, :] = xpad[L:L + halo, :]
    bpad[0:halo, :] = bpad[L:L + halo, :]

    lane = lax.broadcasted_iota(jnp.int32, (1, LANES), 1)
    dtv = dt_ref[...] + dtb_ref[...]
    dt = jnp.maximum(dtv, 0.0) + jnp.log(1.0 + jnp.exp(-jnp.abs(dtv)))
    dt = jnp.where(lane < n_heads, dt, 0.0)
    a = -jnp.exp(alog_ref[...])
    da = dt * a
    row = lax.broadcasted_iota(jnp.int32, (L, L), 0)
    col = lax.broadcasted_iota(jnp.int32, (L, L), 1)
    tril = row >= col
    cs = _dot_exact_lhs(tril.astype(BF16), da)
    cs_t = cs.T
    dt_t = dt.T
    cs_last = cs[L - 1:L, :]
    e = e_ref[...]
    ecs = _dot(jnp.exp(cs).astype(BF16), e)
    dend = _dot((jnp.exp(cs_last - cs) * dt).astype(BF16), e)
    lane_p = lax.broadcasted_iota(jnp.int32, (L, 2 * P), 1)

    ys = []
    for g in range(G):
        bm = bcv[:, g * N:(g + 1) * N]
        cm = bcv[:, G * N + g * N:G * N + (g + 1) * N]
        bm16 = bm.astype(BF16)
        cm16 = cm.astype(BF16)
        cb = _dot_nt(cm16, bm16)
        xg = x[:, g * gw:(g + 1) * gw]
        yd = []
        for hp in range(hg // 2):
            ws = []
            for hh in range(2):
                h = g * hg + hp * 2 + hh
                seg = cs[:, h:h + 1] - cs_t[h:h + 1, :]
                dec = jnp.exp(jnp.where(tril, seg, -jnp.inf))
                ws.append(cb * dec * dt_t[h:h + 1, :])
            wcat = jnp.concatenate(ws, axis=1).astype(BF16)
            xp = xg[:, hp * 2 * P:(hp + 1) * 2 * P]
            xbd = jnp.concatenate([jnp.where(lane_p < P, xp, 0.0),
                                   jnp.where(lane_p >= P, xp, 0.0)], axis=0).astype(BF16)
            yd.append(_dot(wcat, xbd))
        y_diag = jnp.concatenate(yd, axis=1)
        hprev = hst[g]
        y_off = _dot(cm16, hprev.astype(BF16)) * ecs[:, g * gw:(g + 1) * gw]
        st = _dot(bm.T.astype(BF16), (xg * dend[:, g * gw:(g + 1) * gw]).astype(BF16))
        hst[g] = hprev * ecs[L - 1:L, g * gw:(g + 1) * gw] + st
        y = y_diag + y_off + dsk_ref[:, g * gw:(g + 1) * gw] * xg
        y = y * _silu(z_ref[:, g * gw:(g + 1) * gw])
        y = y * lax.rsqrt(jnp.mean(y * y, axis=-1, keepdims=True) + EPS)
        ys.append(y * nw_ref[:, g * gw:(g + 1) * gw])
    o_ref[...] = jnp.concatenate(ys, axis=1).astype(o_ref.dtype)


def _ssd(proj3, cols, cwx, cbx, cwb, cbb, dtb, alog, dsk, nw, e, layer, n_heads):
    b, s, _ = proj3.shape
    width = n_heads * SSD_HEAD_DIM
    bcw = 2 * SSD_GROUPS * SSD_STATE
    L = SSD_CHUNK
    gw = width // SSD_GROUPS

    def wspec(c):
        return pl.BlockSpec((None, 1, c), lambda bi, ci: (layer, 0, 0))

    return pl.pallas_call(
        functools.partial(_ssd_kernel, n_heads=n_heads),
        grid=(b, s // L),
        in_specs=[
            pl.BlockSpec((None, L, width), lambda bi, ci: (bi, ci, cols["z"] // width)),
            pl.BlockSpec((None, L, width), lambda bi, ci: (bi, ci, cols["xs"] // width)),
            pl.BlockSpec((None, L, bcw), lambda bi, ci: (bi, ci, cols["bc"] // bcw)),
            pl.BlockSpec((None, L, LANES), lambda bi, ci: (bi, ci, cols["dt"] // LANES)),
            pl.BlockSpec((None, SSD_CONV, width), lambda bi, ci: (layer, 0, 0)),
            wspec(width),
            pl.BlockSpec((None, SSD_CONV, bcw), lambda bi, ci: (layer, 0, 0)),
            wspec(bcw),
            wspec(LANES),
            wspec(LANES),
            wspec(width),
            wspec(width),
            pl.BlockSpec((LANES, width), lambda bi, ci: (0, 0)),
        ],
        out_specs=pl.BlockSpec((None, L, width), lambda bi, ci: (bi, ci, 0)),
        out_shape=jax.ShapeDtypeStruct((b, s, width), BF16),
        scratch_shapes=[pltpu.VMEM((L + SUBLANES, width), F32),
                        pltpu.VMEM((L + SUBLANES, bcw), F32),
                        pltpu.VMEM((SSD_GROUPS, SSD_STATE, gw), F32)],
        compiler_params=_cparams(("parallel", "arbitrary")),
        name="ssd",
    )(proj3, proj3, proj3, proj3, cwx, cbx, cwb, cbb, dtb, alog, dsk, nw, e)


def _cnv_kernel(glu_ref, w_ref, b_ref, lw_ref, lb_ref, o_ref, pad, shifted):
    T = CONV_TILE
    c = pad.shape[1]

    @pl.when(pl.program_id(1) == 0)
    def _():
        pad[0:CONV_HALO, :] = jnp.zeros((CONV_HALO, c), F32)

    pad[CONV_HALO:CONV_HALO + T, :] = glu_ref[:, 0:c] * jax.nn.sigmoid(glu_ref[:, c:2 * c])
    span = T + CONV_HALO - SUBLANES
    for r in range(1, SUBLANES):
        shifted[r - 1] = pad[r:r + span, :]
    acc = b_ref[...]
    for k in range(CONV_KERNEL):
        off = CONV_HALO - (CONV_KERNEL - 1) + k
        r, base = off % SUBLANES, off - off % SUBLANES
        src = pad[base:base + T, :] if r == 0 else shifted[r - 1, base:base + T, :]
        acc = acc + w_ref[k:k + 1, :] * src
    pad[0:CONV_HALO, :] = pad[T:T + CONV_HALO, :]
    mu = jnp.mean(acc, axis=-1, keepdims=True)
    cen = acc - mu
    var = jnp.mean(cen * cen, axis=-1, keepdims=True)
    v = cen * lax.rsqrt(var + EPS) * lw_ref[...] + lb_ref[...]
    o_ref[...] = _silu(v).astype(o_ref.dtype)


def _cnv(proj3, col, w, bias, lw, lb, layer):
    b, s, _ = proj3.shape
    c = w.shape[2]
    T = CONV_TILE
    return pl.pallas_call(
        _cnv_kernel,
        grid=(b, s // T),
        in_specs=[
            pl.BlockSpec((None, T, 2 * c), lambda bi, ti: (bi, ti, col // (2 * c))),
            pl.BlockSpec((None, CONV_KERNEL, c), lambda bi, ti: (layer, 0, 0)),
            pl.BlockSpec((None, 1, c), lambda bi, ti: (layer, 0, 0)),
            pl.BlockSpec((None, 1, c), lambda bi, ti: (layer, 0, 0)),
            pl.BlockSpec((None, 1, c), lambda bi, ti: (layer, 0, 0)),
        ],
        out_specs=pl.BlockSpec((None, T, c), lambda bi, ti: (bi, ti, 0)),
        out_shape=jax.ShapeDtypeStruct((b, s, c), BF16),
        scratch_shapes=[pltpu.VMEM((T + CONV_HALO, c), F32),
                        pltpu.VMEM((SUBLANES - 1, T + CONV_HALO - SUBLANES, c), F32)],
        compiler_params=_cparams(("parallel", "arbitrary")),
        name="cnv",
    )(proj3, w, bias, lw, lb)


def _cmp_kernel(u_ref, pe_ref, w1_ref, b1_ref, w2_ref, w2t_ref, on_ref, ot_ref):
    u = u_ref[...]
    half = u.shape[1]
    nc = u.shape[0]
    a0 = (u + pe_ref[:, 0:half]).astype(BF16)
    a1 = (u + pe_ref[:, half:2 * half]).astype(BF16)
    h0 = _dot(a0, w1_ref[0:half, :])
    h1 = _dot(a1, w1_ref[half:2 * half, :])
    pre = h0 + pltpu.roll(h1, nc - 1, 0) + b1_ref[...]
    act = _silu(pre).astype(BF16)
    on_ref[...] = _dot(act, w2_ref[...]).astype(on_ref.dtype)
    ot_ref[...] = _dot_nt(w2t_ref[...], act).astype(ot_ref.dtype)


def _cmp(u16, pe, w1, b1, w2, w2t, layer):
    _, b, g, nc, uw = u16.shape
    hid = w1.shape[3]
    dh = w2.shape[3]
    return pl.pallas_call(
        _cmp_kernel,
        grid=(2, b, g),
        in_specs=[
            pl.BlockSpec((None, None, None, nc, uw), lambda k, bi, gi: (k, bi, gi, 0, 0)),
            pl.BlockSpec((None, None, 1, 2 * uw), lambda k, bi, gi: (k, layer, 0, 0)),
            pl.BlockSpec((None, None, 2 * uw, hid), lambda k, bi, gi: (k, layer, 0, 0)),
            pl.BlockSpec((None, None, 1, hid), lambda k, bi, gi: (k, layer, 0, 0)),
            pl.BlockSpec((None, None, hid, dh), lambda k, bi, gi: (k, layer, 0, 0)),
            pl.BlockSpec((None, None, dh, hid), lambda k, bi, gi: (k, layer, 0, 0)),
        ],
        out_specs=[
            pl.BlockSpec((None, None, None, nc, dh), lambda k, bi, gi: (k, bi, gi, 0, 0)),
            pl.BlockSpec((None, None, None, dh, nc), lambda k, bi, gi: (k, bi, gi, 0, 0)),
        ],
        out_shape=[jax.ShapeDtypeStruct((2, b, g, nc, dh), BF16),
                   jax.ShapeDtypeStruct((2, b, g, dh, nc), BF16)],
        compiler_params=_cparams(("parallel", "parallel", "parallel")),
        name="cmp",
    )(u16, pe, w1, b1, w2, w2t)


def _nsa_kernel(q_ref, gl_ref, kc_ref, vct_ref, ks_ref, vst_ref, ksd_ref, vsd_ref, kwd_ref,
                vwd_ref, ovt_ref, tb_ref, o_ref, s_scr, *, n_sel, nqb, gate_lane):
    Q = Q_BLOCK
    dh = NSA_HEAD_DIM
    hpg = q_ref.shape[1] // dh
    ncol = hpg * Q
    nc = kc_ref.shape[0]
    nbp = ovt_ref.shape[0]
    kt = ks_ref.shape[1]
    qi = pl.program_id(2)
    q0 = qi * Q

    qt = q_ref[...] * (dh ** -0.5 * math.log2(math.e))
    q = jnp.concatenate([qt[:, h * dh:(h + 1) * dh] for h in range(hpg)], axis=0).astype(BF16)
    gl_t = gl_ref[...].T
    gates = [jax.nn.sigmoid(jnp.concatenate(
        [gl_t[gate_lane + 3 * h + k:gate_lane + 3 * h + k + 1, :] for h in range(hpg)], axis=1))
        for k in range(3)]
    lane = lax.broadcasted_iota(jnp.int32, (1, ncol), 1)
    tok = q0 + (lane & (Q - 1))

    def tile_cols(a):
        return jnp.concatenate([a] * hpg, axis=1)

    r_i = lax.broadcasted_iota(jnp.int32, (Q, Q), 0)
    c_i = lax.broadcasted_iota(jnp.int32, (Q, Q), 1)
    tri_le = jnp.where(r_i <= c_i, 0.0, NEG)
    tri_gt = jnp.where(r_i > c_i, 0.0, NEG)

    off = pl.multiple_of((nqb - 1 - qi) * (Q // CMP_STRIDE), SUBLANES)
    bias_c = tb_ref[pl.ds(off, nc), :]
    scm = _dot_nt(kc_ref[...], q) + tile_cols(bias_c)
    m_c = jnp.max(scm, axis=0, keepdims=True)
    p_c = jnp.exp2(scm - m_c)
    l_c = jnp.sum(p_c, axis=0, keepdims=True)
    p_c = p_c * jnp.where(tok >= CMP_BLOCK - 1, 1.0 / l_c, 0.0)
    o_c = _dot(vct_ref[...], p_c.astype(BF16))

    psum = p_c[:, 0:Q]
    for h in range(1, hpg):
        psum = psum + p_c[:, h * Q:(h + 1) * Q]
    imp = _dot_exact_lhs(ovt_ref[...], psum)

    nch = WINDOW // Q + 1
    tri_d = tile_cols(tri_le)
    first = qi - (nch - 1)
    idx = [jnp.maximum(first + c, 0) for c in range(nch)]
    before = [jnp.where(first + c >= 0, 0.0, NEG) for c in range(nch)]
    sw = _dot_nt(jnp.concatenate([kwd_ref[i] for i in idx], axis=0), q)
    sw = jnp.concatenate(
        [sw[0:Q] + (tile_cols(tri_gt) + before[0])]
        + [sw[c * Q:(c + 1) * Q] + before[c] for c in range(1, nch - 1)]
        + [sw[(nch - 1) * Q:nch * Q] + tri_d], axis=0)
    m_w = jnp.max(sw, axis=0, keepdims=True)
    p_w = jnp.exp2(sw - m_w).astype(BF16)
    acc_w = jnp.zeros((V_ROWS, ncol), F32)
    for c in range(nch):
        acc_w = acc_w + _dot(vwd_ref[idx[c]], p_w[c * Q:(c + 1) * Q, :])
    o_w = acc_w[0:dh, :] * (1.0 / acc_w[dh:dh + 1, :])
    o_cw = gates[0] * o_c + gates[2] * o_w

    s_d = _dot_nt(ksd_ref[...], q) + tri_d
    m_d = jnp.max(s_d, axis=0, keepdims=True)

    blk = lax.broadcasted_iota(jnp.int32, (nbp, Q), 0)
    t1 = q0 + lax.broadcasted_iota(jnp.int32, (nbp, Q), 1)
    cur = lax.shift_right_logical(t1, SLC_BLOCK.bit_length() - 1)
    forced = (blk == 0) | (blk == cur) | (blk == cur - 1)
    val = jnp.where(blk * SLC_BLOCK <= t1, jnp.where(forced, imp + FORCE_BONUS, imp), NEG)
    blk_f = blk.astype(F32)
    for _ in range(n_sel):
        mx = jnp.max(val, axis=0, keepdims=True)
        idx = jnp.min(jnp.where(val == mx, blk_f, float(nbp)), axis=0, keepdims=True)
        val = jnp.where(blk_f == idx, -jnp.inf, val)
    bias = jnp.where((val == -jnp.inf) & (blk < qi * (Q // SLC_BLOCK)), 0.0, NEG)
    bias_t = tile_cols(bias).T.astype(BF16)
    qa = jnp.concatenate([bias_t, q], axis=1)

    n_trip = q0 // kt + 1

    def score_trip(g, m):
        s = _dot_nt(ks_ref[g], qa)
        s_scr[g] = s
        return jnp.maximum(m, jnp.max(s, axis=0, keepdims=True))

    def paired(trip, carry):
        def quad(i, c):
            for u in range(4):
                c = trip(4 * i + u, c)
            return c

        carry = lax.fori_loop(0, n_trip // 4, quad, carry)
        rem = n_trip % 4
        base = n_trip - rem
        carry = lax.cond(rem >= 2, lambda c: trip(base + 1, trip(base, c)), lambda c: c, carry)
        return lax.cond(rem % 2 == 1, lambda c: trip(n_trip - 1, c), lambda c: c, carry)

    m_s = paired(score_trip, m_d)

    def value_trip(g, acc):
        p = jnp.exp2(s_scr[g] - m_s).astype(BF16)
        return acc + _dot(vst_ref[g], p)

    acc_s = _dot(vsd_ref[...], jnp.exp2(s_d - m_s).astype(BF16))
    acc_s = paired(value_trip, acc_s)
    o_s = acc_s[0:dh, :] * (1.0 / acc_s[dh:dh + 1, :])

    o_t = o_cw + gates[1] * o_s
    per = LANES // dh
    outs = []
    for hp in range(hpg // per):
        stack = jnp.concatenate([o_t[:, (hp * per + k) * Q:(hp * per + k + 1) * Q]
                                 for k in range(per)], axis=0)
        outs.append(stack.T)
    o_ref[...] = jnp.concatenate(outs, axis=1).astype(o_ref.dtype)


def _nsa(proj3, q_col, gate_col, gate_lane, hpg, cn, ct, ks, vst, ksd, vsd, kwd, vwd, ovt, tb, n_sel):
    b, s, _ = proj3.shape
    g, nqb, dh = ks.shape[1], ksd.shape[2], NSA_HEAD_DIM
    nc = cn.shape[3]
    nt, kt, ka = ks.shape[2], ks.shape[3], ks.shape[4]
    nbp = ovt.shape[0]
    ncol = hpg * Q_BLOCK
    qw = hpg * dh
    vr = vst.shape[3]
    once = pl.Buffered(1)
    return pl.pallas_call(
        functools.partial(_nsa_kernel, n_sel=n_sel, nqb=nqb, gate_lane=gate_lane),
        grid=(b, g, nqb),
        in_specs=[
            pl.BlockSpec((None, Q_BLOCK, qw), lambda bi, gi, qi: (bi, qi, q_col // qw + gi)),
            pl.BlockSpec((None, Q_BLOCK, LANES), lambda bi, gi, qi: (bi, qi, gate_col // LANES + gi)),
            pl.BlockSpec((None, None, None, nc, dh), lambda bi, gi, qi: (0, bi, gi, 0, 0)),
            pl.BlockSpec((None, None, None, dh, nc), lambda bi, gi, qi: (1, bi, gi, 0, 0)),
            pl.BlockSpec((None, None, nt, kt, ka), lambda bi, gi, qi: (bi, gi, 0, 0, 0),
                         pipeline_mode=once),
            pl.BlockSpec((None, None, nt, vr, kt), lambda bi, gi, qi: (bi, gi, 0, 0, 0),
                         pipeline_mode=once),
            pl.BlockSpec((None, None, None, Q_BLOCK, dh), lambda bi, gi, qi: (bi, gi, qi, 0, 0)),
            pl.BlockSpec((None, None, None, vr, Q_BLOCK), lambda bi, gi, qi: (bi, gi, qi, 0, 0)),
            pl.BlockSpec((None, None, nqb, Q_BLOCK, dh), lambda bi, gi, qi: (bi, gi, 0, 0, 0),
                         pipeline_mode=once),
            pl.BlockSpec((None, None, nqb, vr, Q_BLOCK), lambda bi, gi, qi: (bi, gi, 0, 0, 0),
                         pipeline_mode=once),
            pl.BlockSpec((nbp, nc), lambda bi, gi, qi: (0, 0)),
            pl.BlockSpec(tb.shape, lambda bi, gi, qi: (0, 0)),
        ],
        out_specs=pl.BlockSpec((None, Q_BLOCK, qw), lambda bi, gi, qi: (bi, qi, gi)),
        out_shape=jax.ShapeDtypeStruct((b, s, g * qw), BF16),
        scratch_shapes=[pltpu.VMEM((nt, kt, ncol), F32)],
        compiler_params=_cparams(("parallel", "parallel", "arbitrary")),
        name="nsa",
    )(proj3, proj3, cn, ct, ks, vst, ksd, vsd, kwd, vwd, ovt, tb)


def _kvprep_kernel(pc_ref, ps_ref, pw_ref, craw_ref, ks_ref, vst_ref, ksd_ref, vsd_ref,
                   kwd_ref, vwd_ref):
    dh = NSA_HEAD_DIM
    Q = Q_BLOCK
    kt = ps_ref.shape[0]
    nbp = ks_ref.shape[1] - dh
    j = pl.program_id(2)
    for p in range(CMP_STRIDE):
        rows = pc_ref[pl.ds(p, kt // CMP_STRIDE, stride=CMP_STRIDE), :]
        craw_ref[0, :, p * dh:(p + 1) * dh] = rows[:, 0:dh]
        craw_ref[1, :, p * dh:(p + 1) * dh] = rows[:, dh:2 * dh]

    tail_row = lax.broadcasted_iota(jnp.int32, (V_ROWS - dh, kt), 0)
    tail = jnp.where(tail_row == 0, 1.0, 0.0).astype(BF16)

    def k_and_vt(p_ref):
        p = p_ref[...]
        k = p[:, 0:dh].astype(BF16)
        vt = jnp.concatenate([p.T[dh:2 * dh, :].astype(BF16), tail], axis=0)
        return k, vt

    k_s, vt_s = k_and_vt(ps_ref)
    key = lax.broadcasted_iota(jnp.int32, (kt, nbp), 0)
    bcol = lax.broadcasted_iota(jnp.int32, (kt, nbp), 1)
    blk_of_key = j * (kt // SLC_BLOCK) + lax.shift_right_logical(key, SLC_BLOCK.bit_length() - 1)
    ks_ref[:, 0:nbp] = jnp.where(bcol == blk_of_key, 1.0, 0.0).astype(BF16)
    ks_ref[:, nbp:nbp + dh] = k_s
    vst_ref[...] = vt_s
    k_w, vt_w = k_and_vt(pw_ref)
    for c in range(kt // Q):
        ksd_ref[c] = k_s[c * Q:(c + 1) * Q, :]
        vsd_ref[c] = vt_s[:, c * Q:(c + 1) * Q]
        kwd_ref[c] = k_w[c * Q:(c + 1) * Q, :]
        vwd_ref[c] = vt_w[:, c * Q:(c + 1) * Q]


def _kvprep(proj3, kv_col, g, kt, nbp):
    b, s, _ = proj3.shape
    dh = NSA_HEAD_DIM
    nt, nqb, cpt = s // kt, s // Q_BLOCK, kt // Q_BLOCK
    pair = 2 * dh
    base = kv_col // pair

    def pspec(branch):
        return pl.BlockSpec((None, kt, pair), lambda bi, gi, j: (bi, j, base + 3 * gi + branch))

    chunk_k = pl.BlockSpec((None, None, cpt, Q_BLOCK, dh), lambda bi, gi, j: (bi, gi, j, 0, 0))
    chunk_v = pl.BlockSpec((None, None, cpt, V_ROWS, Q_BLOCK), lambda bi, gi, j: (bi, gi, j, 0, 0))
    return pl.pallas_call(
        _kvprep_kernel,
        grid=(b, g, nt),
        in_specs=[pspec(0), pspec(1), pspec(2)],
        out_specs=[
            pl.BlockSpec((2, None, None, kt // CMP_STRIDE, CMP_STRIDE * dh),
                         lambda bi, gi, j: (0, bi, gi, j, 0)),
            pl.BlockSpec((None, None, None, kt, nbp + dh), lambda bi, gi, j: (bi, gi, j, 0, 0)),
            pl.BlockSpec((None, None, None, V_ROWS, kt), lambda bi, gi, j: (bi, gi, j, 0, 0)),
            chunk_k, chunk_v, chunk_k, chunk_v,
        ],
        out_shape=[
            jax.ShapeDtypeStruct((2, b, g, s // CMP_STRIDE, CMP_STRIDE * dh), F32),
            jax.ShapeDtypeStruct((b, g, nt, kt, nbp + dh), BF16),
            jax.ShapeDtypeStruct((b, g, nt, V_ROWS, kt), BF16),
            jax.ShapeDtypeStruct((b, g, nqb, Q_BLOCK, dh), BF16),
            jax.ShapeDtypeStruct((b, g, nqb, V_ROWS, Q_BLOCK), BF16),
            jax.ShapeDtypeStruct((b, g, nqb, Q_BLOCK, dh), BF16),
            jax.ShapeDtypeStruct((b, g, nqb, V_ROWS, Q_BLOCK), BF16),
        ],
        compiler_params=_cparams(("parallel", "parallel", "parallel")),
        name="kvprep",
    )(proj3, proj3, proj3)


def _outproj_kernel(h_ref, ys_ref, yn_ref, yc_ref, w_ref, nw_ref, o_ref):
    w0 = ys_ref.shape[1]
    w1 = w0 + yn_ref.shape[1]
    w2 = w1 + yc_ref.shape[1]
    m = (_dot(ys_ref[...], w_ref[0:w0, :]) + _dot(yn_ref[...], w_ref[w0:w1, :])
         + _dot(yc_ref[...], w_ref[w1:w2, :]))
    o_ref[...] = h_ref[...] + _rms(m, nw_ref[...])


def _outproj(h, ys, yn, yc, w, nw, layer):
    n, d = h.shape
    dm = w.shape[1]
    return pl.pallas_call(
        _outproj_kernel,
        grid=(n // TOKEN_TILE,),
        in_specs=[
            pl.BlockSpec((TOKEN_TILE, d), lambda i: (i, 0)),
            pl.BlockSpec((TOKEN_TILE, ys.shape[1]), lambda i: (i, 0)),
            pl.BlockSpec((TOKEN_TILE, yn.shape[1]), lambda i: (i, 0)),
            pl.BlockSpec((TOKEN_TILE, yc.shape[1]), lambda i: (i, 0)),
            pl.BlockSpec((None, dm, d), lambda i: (layer, 0, 0)),
            pl.BlockSpec((None, 1, d), lambda i: (layer, 0, 0)),
        ],
        out_specs=pl.BlockSpec((TOKEN_TILE, d), lambda i: (i, 0)),
        out_shape=jax.ShapeDtypeStruct((n, d), F32),
        compiler_params=_cparams(("parallel",)),
        name="outproj",
    )(h, ys, yn, yc, w, nw)


def _nsa_constants(nc, nb, nbp, nqb):
    c0 = np.arange(nc)[None, :] * CMP_STRIDE
    s0 = np.arange(nbp)[:, None] * SLC_BLOCK
    ov = np.maximum(np.minimum(c0 + CMP_BLOCK, s0 + SLC_BLOCK) - np.maximum(c0, s0), 0) / CMP_STRIDE
    ov[nb:] = 0.0
    upq = Q_BLOCK // CMP_STRIDE
    r = np.arange(nc + upq * (nqb - 1))[:, None]
    tl = np.arange(Q_BLOCK)[None, :]
    tb = np.where(CMP_STRIDE * r + (CMP_BLOCK - 1) - Q_BLOCK * (nqb - 1) <= tl, 0.0, NEG)
    return jnp.asarray(ov, BF16), jnp.asarray(tb, F32)


def kernel(x, ffn1_pre_norm, ffn1_w_gu, ffn1_w_down, ffn1_post_norm, mix_pre_norm, w_in,
           ssd_conv_w, ssd_conv_b, ssd_dt_bias, ssd_a_log, ssd_d, ssd_norm_w,
           nsa_k_pe, nsa_k_w1, nsa_k_b1, nsa_k_w2, nsa_v_pe, nsa_v_w1, nsa_v_b1, nsa_v_w2,
           cnv_dw_w, cnv_dw_b, cnv_ln_w, cnv_ln_b, w_out, mix_post_norm,
           ffn2_pre_norm, ffn2_w_gu, ffn2_w_down, ffn2_post_norm):
    bsz, s, d = x.shape
    depth = w_in.shape[0]
    n = bsz * s
    n_heads = ssd_dt_bias.shape[1]
    ssd_w = n_heads * SSD_HEAD_DIM
    bcw = 2 * SSD_GROUPS * SSD_STATE
    cch = cnv_dw_w.shape[2]
    d_in = w_in.shape[2]
    kvw = NSA_KV_GROUPS * NSA_HEAD_DIM
    n_gate = d_in - (2 * ssd_w + bcw + n_heads + 6 * kvw + 2 * cch)
    nsa_heads = n_gate // (NSA_HEAD_DIM + 3)
    nsa_w = nsa_heads * NSA_HEAD_DIM
    G = NSA_KV_GROUPS
    hpg = nsa_heads // G
    dh = NSA_HEAD_DIM

    o_z, o_xbc, o_dt = 0, ssd_w, ssd_w + ssd_w + bcw
    o_q = o_dt + n_heads
    o_kv = o_q + nsa_w
    o_gl = o_kv + 6 * kvw
    o_glu = o_gl + 3 * nsa_heads
    gl_w = 3 * hpg
    src = [("z", o_z, ssd_w), ("glu", o_glu, 2 * cch), ("xs", o_xbc, ssd_w),
           ("bc", o_xbc + ssd_w, bcw), ("q", o_q, nsa_w)]
    cols, pos, pieces = {}, 0, []
    for name, o, w in src:
        cols[name] = pos
        pieces.append(w_in[:, :, o:o + w].astype(BF16))
        pos += w
    kv_cols = w_in[:, :, o_kv:o_kv + 6 * kvw].astype(BF16).reshape(depth, d, 6, G, dh)
    pieces.append(kv_cols.transpose(0, 1, 3, 2, 4).reshape(depth, d, 6 * kvw))
    cols["kv"] = pos
    pos += 6 * kvw
    zeros = lambda w: jnp.zeros((depth, d, w), BF16)
    cols["dt"] = cols["gate"] = pos
    assert pos % LANES == 0 and n_heads + gl_w <= LANES
    for gi in range(G):
        head = w_in[:, :, o_dt:o_dt + n_heads].astype(BF16) if gi == 0 else zeros(n_heads)
        pieces += [head, w_in[:, :, o_gl + gi * gl_w:o_gl + (gi + 1) * gl_w].astype(BF16),
                   zeros(LANES - n_heads - gl_w)]
        pos += LANES
    col_tile = 10 * LANES
    total = -(-pos // col_tile) * col_tile
    if total > pos:
        pieces.append(zeros(total - pos))
    w_in_r = jnp.concatenate(pieces, axis=2)
    assert cols["z"] % ssd_w == 0 and cols["xs"] % ssd_w == 0 and cols["bc"] % bcw == 0
    assert cols["glu"] % (2 * cch) == 0 and cols["q"] % (hpg * dh) == 0 and cols["kv"] % (2 * dh) == 0

    r3 = lambda a: a.reshape(depth, 1, -1)
    w1_gu, w1_dn = ffn1_w_gu.astype(BF16), ffn1_w_down.astype(BF16)
    w2_gu, w2_dn = ffn2_w_gu.astype(BF16), ffn2_w_down.astype(BF16)
    w_out16 = w_out.astype(BF16)
    cwx, cwb = ssd_conv_w[:, :, :ssd_w], ssd_conv_w[:, :, ssd_w:]
    cbx, cbb = r3(ssd_conv_b[:, :ssd_w]), r3(ssd_conv_b[:, ssd_w:])
    padl = lambda a: jnp.pad(a, ((0, 0), (0, LANES - a.shape[1]))).reshape(depth, 1, LANES)
    dtb, alog = padl(ssd_dt_bias), padl(ssd_a_log)
    dsk = r3(jnp.repeat(ssd_d, SSD_HEAD_DIM, axis=1))
    e_np = np.zeros((LANES, ssd_w), np.float32)
    for hh in range(n_heads):
        e_np[hh, hh * SSD_HEAD_DIM:(hh + 1) * SSD_HEAD_DIM] = 1.0
    e_mat = jnp.asarray(e_np, dtype=BF16)
    pe = jnp.stack([nsa_k_pe, nsa_v_pe]).reshape(2, depth, 1, CMP_BLOCK * dh)
    cw1 = jnp.stack([nsa_k_w1, nsa_v_w1]).astype(BF16)
    cb1 = jnp.stack([nsa_k_b1, nsa_v_b1]).reshape(2, depth, 1, -1)
    cw2 = jnp.stack([nsa_k_w2, nsa_v_w2]).astype(BF16)
    cw2t = jnp.swapaxes(cw2, 2, 3)
    nc = s // CMP_STRIDE
    nb = s // SLC_BLOCK
    nbp = -(-nb // LANES) * LANES
    nqb = s // Q_BLOCK
    kt = min(KV_TILE, s)
    nt = s // kt
    n_sel = min(SLC_TOPN, nb)
    ovt, tb = _nsa_constants(nc, nb, nbp, nqb)

    def nsa_branch(proj3, l):
        u16, ks, vst, ksd, vsd, kwd, vwd = _kvprep(proj3, cols["kv"], G, kt, nbp)
        cn, ct = _cmp(u16, pe, cw1, cb1, cw2, cw2t, l)
        return _nsa(proj3, cols["q"], cols["gate"], n_heads, hpg, cn, ct, ks, vst, ksd, vsd,
                    kwd, vwd, ovt, tb, n_sel)

    h = x.reshape(n, d)
    for l in range(depth):
        h = _ffn(h, r3(ffn1_pre_norm), w1_gu, w1_dn, r3(ffn1_post_norm), l)

        proj = _inproj(h, r3(mix_pre_norm), w_in_r, l, col_tile)
        proj3 = proj.reshape(bsz, s, total)
        y_ssd = _ssd(proj3, cols, cwx, cbx, cwb, cbb, dtb, alog, dsk, r3(ssd_norm_w), e_mat, l, n_heads)
        y_cnv = _cnv(proj3, cols["glu"], cnv_dw_w, r3(cnv_dw_b), r3(cnv_ln_w), r3(cnv_ln_b), l)
        y_nsa = nsa_branch(proj3, l)

        h = _outproj(h, y_ssd.reshape(n, ssd_w), y_nsa.reshape(n, nsa_w), y_cnv.reshape(n, cch),
                     w_out16, r3(mix_post_norm), l)
        h = _ffn(h, r3(ffn2_pre_norm), w2_gu, w2_dn, r3(ffn2_post_norm), l)
    return h.reshape(bsz, s, d)
```

```python
import functools
import math

import numpy as np
import jax
import jax.numpy as jnp
from jax import lax
from jax.experimental import pallas as pl
from jax.experimental.pallas import tpu as pltpu

F32 = jnp.float32
BF16 = jnp.bfloat16

SSD_HEAD_DIM = 64
SSD_GROUPS = 2
SSD_STATE = 128
SSD_CONV = 4
SSD_CHUNK = 128
NSA_HEAD_DIM = 64
NSA_KV_GROUPS = 2
CMP_BLOCK = 32
CMP_STRIDE = 16
SLC_BLOCK = 64
SLC_TOPN = 16
WINDOW = 512
Q_BLOCK = 128
FORCE_BONUS = 1000.0
CONV_KERNEL = 31
HALF = 0.5
EPS = 1e-6
NEG = -1e30

LANES = 128
SUBLANES = 8
BF16_ROWS = 16
VMEM_LIMIT = 56 * 1024 * 1024
TOKEN_TILE = 512
PROJ_TOKEN_TILE = 1024
FF_TILE = 512
KV_TILE = 1024
CONV_TILE = 256
CONV_HALO = 32
V_ROWS = NSA_HEAD_DIM + BF16_ROWS


def _cparams(sem):
    return pltpu.CompilerParams(dimension_semantics=sem, vmem_limit_bytes=VMEM_LIMIT)


def _rms(x, w):
    return (x * lax.rsqrt(jnp.mean(x * x, axis=-1, keepdims=True) + EPS)) * w


def _silu(x):
    return x * jax.nn.sigmoid(x)


def _dot(a, b):
    return jnp.dot(a, b, preferred_element_type=F32)


def _dot_nt(a, b):
    return lax.dot_general(a, b, (((1,), (1,)), ((), ())), preferred_element_type=F32)


def _split3(x):
    hi = x.astype(BF16)
    r1 = x - hi.astype(F32)
    mid = r1.astype(BF16)
    lo = (r1 - mid.astype(F32)).astype(BF16)
    return hi, mid, lo


def _dot_exact_lhs(a, x):
    hi, mid, lo = _split3(x)
    return _dot(a, hi) + _dot(a, mid) + _dot(a, lo)


def _ffn_kernel(h_ref, pre_ref, wg_ref, wu_ref, wd_ref, post_ref, o_ref, u_scr, acc_scr, *, nf):
    j = pl.program_id(1)
    last = nf - 1

    def step(first, final):
        if first:
            u = _rms(h_ref[...], pre_ref[...]).astype(BF16)
            u_scr[...] = u
        else:
            u = u_scr[...]
        g = _dot(u, wg_ref[...])
        v = _dot(u, wu_ref[...])
        part = _dot((_silu(g) * v).astype(BF16), wd_ref[...])
        acc = part if first else acc_scr[...] + part
        if final:
            o_ref[...] = h_ref[...] + HALF * _rms(acc, post_ref[...])
        else:
            acc_scr[...] = acc

    if nf == 1:
        step(True, True)
    else:
        pl.when(j == 0)(lambda: step(True, False))
        pl.when((j > 0) & (j < last))(lambda: step(False, False))
        pl.when(j == last)(lambda: step(False, True))


def _ffn(h, pre_w, w_gu, w_down, post_w, layer):
    n, d = h.shape
    f = w_down.shape[1]
    nf = f // FF_TILE
    return pl.pallas_call(
        functools.partial(_ffn_kernel, nf=nf),
        grid=(n // TOKEN_TILE, nf),
        in_specs=[
            pl.BlockSpec((TOKEN_TILE, d), lambda i, j: (i, 0)),
            pl.BlockSpec((None, 1, d), lambda i, j: (layer, 0, 0)),
            pl.BlockSpec((None, d, FF_TILE), lambda i, j: (layer, 0, j)),
            pl.BlockSpec((None, d, FF_TILE), lambda i, j: (layer, 0, j + nf)),
            pl.BlockSpec((None, FF_TILE, d), lambda i, j: (layer, j, 0)),
            pl.BlockSpec((None, 1, d), lambda i, j: (layer, 0, 0)),
        ],
        out_specs=pl.BlockSpec((TOKEN_TILE, d), lambda i, j: (i, 0)),
        out_shape=jax.ShapeDtypeStruct((n, d), F32),
        scratch_shapes=[pltpu.VMEM((TOKEN_TILE, d), BF16), pltpu.VMEM((TOKEN_TILE, d), F32)],
        compiler_params=_cparams(("parallel", "arbitrary")),
        name="ffn",
    )(h, pre_w, w_gu, w_gu, w_down, post_w)


def _inproj_kernel(h_ref, nw_ref, w_ref, o_ref, u_scr):
    @pl.when(pl.program_id(1) == 0)
    def _():
        u = _rms(h_ref[...], nw_ref[...]).astype(BF16)
        u_scr[...] = u
        o_ref[...] = _dot(u, w_ref[...])

    @pl.when(pl.program_id(1) > 0)
    def _():
        o_ref[...] = _dot(u_scr[...], w_ref[...])


def _inproj(h, nw, w, layer, col_tile):
    n, d = h.shape
    cols = w.shape[2]
    return pl.pallas_call(
        _inproj_kernel,
        grid=(n // PROJ_TOKEN_TILE, cols // col_tile),
        in_specs=[
            pl.BlockSpec((PROJ_TOKEN_TILE, d), lambda i, j: (i, 0)),
            pl.BlockSpec((None, 1, d), lambda i, j: (layer, 0, 0)),
            pl.BlockSpec((None, d, col_tile), lambda i, j: (layer, 0, j)),
        ],
        out_specs=pl.BlockSpec((PROJ_TOKEN_TILE, col_tile), lambda i, j: (i, j)),
        out_shape=jax.ShapeDtypeStruct((n, cols), F32),
        scratch_shapes=[pltpu.VMEM((PROJ_TOKEN_TILE, d), BF16)],
        compiler_params=_cparams(("parallel", "arbitrary")),
        name="inproj",
    )(h, nw, w)


def _ssd_kernel(z_ref, xs_ref, bc_ref, dt_ref, cwx_ref, cbx_ref, cwb_ref, cbb_ref, dtb_ref,
                alog_ref, dsk_ref, nw_ref, e_ref, o_ref, xpad, bpad, hst, *, n_heads):
    L = SSD_CHUNK
    N = SSD_STATE
    P = SSD_HEAD_DIM
    G = SSD_GROUPS
    hg = n_heads // G
    gw = hg * P
    halo = SUBLANES

    @pl.when(pl.program_id(1) == 0)
    def _():
        xpad[0:halo, :] = jnp.zeros((halo, xpad.shape[1]), F32)
        bpad[0:halo, :] = jnp.zeros((halo, bpad.shape[1]), F32)
        hst[...] = jnp.zeros_like(hst)

    xpad[halo:halo + L, :] = xs_ref[...]
    bpad[halo:halo + L, :] = bc_ref[...]

    def conv_silu(pad, w_ref, b_ref):
        acc = b_ref[...]
        for k in range(SSD_CONV):
            off = halo - (SSD_CONV - 1) + k
            acc = acc + w_ref[k:k + 1, :] * pad[off:off + L, :]
        return _silu(acc)

    x = conv_silu(xpad, cwx_ref, cbx_ref)
    bcv = conv_silu(bpad, cwb_ref, cbb_ref)
    xpad[0:halo, :] = xpad[L:L + halo, :]
    bpad[0:halo, :] = bpad[L:L + halo, :]

    lane = lax.broadcasted_iota(jnp.int32, (1, LANES), 1)
    dtv = dt_ref[...] + dtb_ref[...]
    dt = jnp.maximum(dtv, 0.0) + jnp.log(1.0 + jnp.exp(-jnp.abs(dtv)))
    dt = jnp.where(lane < n_heads, dt, 0.0)
    a = -jnp.exp(alog_ref[...])
    da = dt * a
    row = lax.broadcasted_iota(jnp.int32, (L, L), 0)
    col = lax.broadcasted_iota(jnp.int32, (L, L), 1)
    tril = row >= col
    cs = _dot_exact_lhs(tril.astype(BF16), da)
    cs_t = cs.T
    dt_t = dt.T
    cs_last = cs[L - 1:L, :]
    e = e_ref[...]
    ecs = _dot(jnp.exp(cs).astype(BF16), e)
    dend = _dot((jnp.exp(cs_last - cs) * dt).astype(BF16), e)
    lane_p = lax.broadcasted_iota(jnp.int32, (L, 2 * P), 1)

    ys = []
    for g in range(G):
        bm = bcv[:, g * N:(g + 1) * N]
        cm = bcv[:, G * N + g * N:G * N + (g + 1) * N]
        bm16 = bm.astype(BF16)
        cm16 = cm.astype(BF16)
        cb = _dot_nt(cm16, bm16)
        xg = x[:, g * gw:(g + 1) * gw]
        yd = []
        for hp in range(hg // 2):
            ws = []
            for hh in range(2):
                h = g * hg + hp * 2 + hh
                seg = cs[:, h:h + 1] - cs_t[h:h + 1, :]
                dec = jnp.exp(jnp.where(tril, seg, -jnp.inf))
                ws.append(cb * dec * dt_t[h:h + 1, :])
            wcat = jnp.concatenate(ws, axis=1).astype(BF16)
            xp = xg[:, hp * 2 * P:(hp + 1) * 2 * P]
            xbd = jnp.concatenate([jnp.where(lane_p < P, xp, 0.0),
                                   jnp.where(lane_p >= P, xp, 0.0)], axis=0).astype(BF16)
            yd.append(_dot(wcat, xbd))
        y_diag = jnp.concatenate(yd, axis=1)
        hprev = hst[g]
        y_off = _dot(cm16, hprev.astype(BF16)) * ecs[:, g * gw:(g + 1) * gw]
        st = _dot(bm.T.astype(BF16), (xg * dend[:, g * gw:(g + 1) * gw]).astype(BF16))
        hst[g] = hprev * ecs[L - 1:L, g * gw:(g + 1) * gw] + st
        y = y_diag + y_off + dsk_ref[:, g * gw:(g + 1) * gw] * xg
        y = y * _silu(z_ref[:, g * gw:(g + 1) * gw])
        y = y * lax.rsqrt(jnp.mean(y * y, axis=-1, keepdims=True) + EPS)
        ys.append(y * nw_ref[:, g * gw:(g + 1) * gw])
    o_ref[...] = jnp.concatenate(ys, axis=1).astype(o_ref.dtype)


def _ssd(proj3, cols, cwx, cbx, cwb, cbb, dtb, alog, dsk, nw, e, layer, n_heads):
    b, s, _ = proj3.shape
    width = n_heads * SSD_HEAD_DIM
    bcw = 2 * SSD_GROUPS * SSD_STATE
    L = SSD_CHUNK
    gw = width // SSD_GROUPS

    def wspec(c):
        return pl.BlockSpec((None, 1, c), lambda bi, ci: (layer, 0, 0))

    return pl.pallas_call(
        functools.partial(_ssd_kernel, n_heads=n_heads),
        grid=(b, s // L),
        in_specs=[
            pl.BlockSpec((None, L, width), lambda bi, ci: (bi, ci, cols["z"] // width)),
            pl.BlockSpec((None, L, width), lambda bi, ci: (bi, ci, cols["xs"] // width)),
            pl.BlockSpec((None, L, bcw), lambda bi, ci: (bi, ci, cols["bc"] // bcw)),
            pl.BlockSpec((None, L, LANES), lambda bi, ci: (bi, ci, cols["dt"] // LANES)),
            pl.BlockSpec((None, SSD_CONV, width), lambda bi, ci: (layer, 0, 0)),
            wspec(width),
            pl.BlockSpec((None, SSD_CONV, bcw), lambda bi, ci: (layer, 0, 0)),
            wspec(bcw),
            wspec(LANES),
            wspec(LANES),
            wspec(width),
            wspec(width),
            pl.BlockSpec((LANES, width), lambda bi, ci: (0, 0)),
        ],
        out_specs=pl.BlockSpec((None, L, width), lambda bi, ci: (bi, ci, 0)),
        out_shape=jax.ShapeDtypeStruct((b, s, width), BF16),
        scratch_shapes=[pltpu.VMEM((L + SUBLANES, width), F32),
                        pltpu.VMEM((L + SUBLANES, bcw), F32),
                        pltpu.VMEM((SSD_GROUPS, SSD_STATE, gw), F32)],
        compiler_params=_cparams(("parallel", "arbitrary")),
        name="ssd",
    )(proj3, proj3, proj3, proj3, cwx, cbx, cwb, cbb, dtb, alog, dsk, nw, e)


def _cnv_kernel(glu_ref, w_ref, b_ref, lw_ref, lb_ref, o_ref, pad, shifted):
    T = CONV_TILE
    c = pad.shape[1]

    @pl.when(pl.program_id(1) == 0)
    def _():
        pad[0:CONV_HALO, :] = jnp.zeros((CONV_HALO, c), F32)

    pad[CONV_HALO:CONV_HALO + T, :] = glu_ref[:, 0:c] * jax.nn.sigmoid(glu_ref[:, c:2 * c])
    span = T + CONV_HALO - SUBLANES
    for r in range(1, SUBLANES):
        shifted[r - 1] = pad[r:r + span, :]
    acc = b_ref[...]
    for k in range(CONV_KERNEL):
        off = CONV_HALO - (CONV_KERNEL - 1) + k
        r, base = off % SUBLANES, off - off % SUBLANES
        src = pad[base:base + T, :] if r == 0 else shifted[r - 1, base:base + T, :]
        acc = acc + w_ref[k:k + 1, :] * src
    pad[0:CONV_HALO, :] = pad[T:T + CONV_HALO, :]
    mu = jnp.mean(acc, axis=-1, keepdims=True)
    cen = acc - mu
    var = jnp.mean(cen * cen, axis=-1, keepdims=True)
    v = cen * lax.rsqrt(var + EPS) * lw_ref[...] + lb_ref[...]
    o_ref[...] = _silu(v).astype(o_ref.dtype)


def _cnv(proj3, col, w, bias, lw, lb, layer):
    b, s, _ = proj3.shape
    c = w.shape[2]
    T = CONV_TILE
    return pl.pallas_call(
        _cnv_kernel,
        grid=(b, s // T),
        in_specs=[
            pl.BlockSpec((None, T, 2 * c), lambda bi, ti: (bi, ti, col // (2 * c))),
            pl.BlockSpec((None, CONV_KERNEL, c), lambda bi, ti: (layer, 0, 0)),
            pl.BlockSpec((None, 1, c), lambda bi, ti: (layer, 0, 0)),
            pl.BlockSpec((None, 1, c), lambda bi, ti: (layer, 0, 0)),
            pl.BlockSpec((None, 1, c), lambda bi, ti: (layer, 0, 0)),
        ],
        out_specs=pl.BlockSpec((None, T, c), lambda bi, ti: (bi, ti, 0)),
        out_shape=jax.ShapeDtypeStruct((b, s, c), BF16),
        scratch_shapes=[pltpu.VMEM((T + CONV_HALO, c), F32),
                        pltpu.VMEM((SUBLANES - 1, T + CONV_HALO - SUBLANES, c), F32)],
        compiler_params=_cparams(("parallel", "arbitrary")),
        name="cnv",
    )(proj3, w, bias, lw, lb)


def _cmp_kernel(u_ref, pe_ref, w1_ref, b1_ref, w2_ref, w2t_ref, on_ref, ot_ref):
    u = u_ref[...]
    half = u.shape[1]
    nc = u.shape[0]
    a0 = (u + pe_ref[:, 0:half]).astype(BF16)
    a1 = (u + pe_ref[:, half:2 * half]).astype(BF16)
    h0 = _dot(a0, w1_ref[0:half, :])
    h1 = _dot(a1, w1_ref[half:2 * half, :])
    pre = h0 + pltpu.roll(h1, nc - 1, 0) + b1_ref[...]
    act = _silu(pre).astype(BF16)
    on_ref[...] = _dot(act, w2_ref[...]).astype(on_ref.dtype)
    ot_ref[...] = _dot_nt(w2t_ref[...], act).astype(ot_ref.dtype)


def _cmp(u16, pe, w1, b1, w2, w2t, layer):
    _, b, g, nc, uw = u16.shape
    hid = w1.shape[3]
    dh = w2.shape[3]
    return pl.pallas_call(
        _cmp_kernel,
        grid=(2, b, g),
        in_specs=[
            pl.BlockSpec((None, None, None, nc, uw), lambda k, bi, gi: (k, bi, gi, 0, 0)),
            pl.BlockSpec((None, None, 1, 2 * uw), lambda k, bi, gi: (k, layer, 0, 0)),
            pl.BlockSpec((None, None, 2 * uw, hid), lambda k, bi, gi: (k, layer, 0, 0)),
            pl.BlockSpec((None, None, 1, hid), lambda k, bi, gi: (k, layer, 0, 0)),
            pl.BlockSpec((None, None, hid, dh), lambda k, bi, gi: (k, layer, 0, 0)),
            pl.BlockSpec((None, None, dh, hid), lambda k, bi, gi: (k, layer, 0, 0)),
        ],
        out_specs=[
            pl.BlockSpec((None, None, None, nc, dh), lambda k, bi, gi: (k, bi, gi, 0, 0)),
            pl.BlockSpec((None, None, None, dh, nc), lambda k, bi, gi: (k, bi, gi, 0, 0)),
        ],
        out_shape=[jax.ShapeDtypeStruct((2, b, g, nc, dh), BF16),
                   jax.ShapeDtypeStruct((2, b, g, dh, nc), BF16)],
        compiler_params=_cparams(("parallel", "parallel", "parallel")),
        name="cmp",
    )(u16, pe, w1, b1, w2, w2t)


def _nsa_kernel(q_ref, gl_ref, kc_ref, vct_ref, ks_ref, vst_ref, ksd_ref, vsd_ref, kwd_ref,
                vwd_ref, tb_ref, o_ref, s_scr, psum_scr, *, n_sel, nqb, gate_lane):
    Q = Q_BLOCK
    dh = NSA_HEAD_DIM
    hpg = q_ref.shape[1] // dh
    ncol = hpg * Q
    nc = kc_ref.shape[0]
    kt = ks_ref.shape[1]
    nbp = ks_ref.shape[2] - dh
    qi = pl.program_id(2)
    q0 = qi * Q

    qt = q_ref[...] * (dh ** -0.5 * math.log2(math.e))
    q = jnp.concatenate([qt[:, h * dh:(h + 1) * dh] for h in range(hpg)], axis=0).astype(BF16)
    gl_t = gl_ref[...].T
    gates = [jax.nn.sigmoid(jnp.concatenate(
        [gl_t[gate_lane + 3 * h + k:gate_lane + 3 * h + k + 1, :] for h in range(hpg)], axis=1))
        for k in range(3)]
    lane = lax.broadcasted_iota(jnp.int32, (1, ncol), 1)
    tok = q0 + (lane & (Q - 1))

    def tile_cols(a):
        return jnp.concatenate([a] * hpg, axis=1)

    r_i = lax.broadcasted_iota(jnp.int32, (Q, Q), 0)
    c_i = lax.broadcasted_iota(jnp.int32, (Q, Q), 1)
    tri_le = jnp.where(r_i <= c_i, 0.0, NEG)
    tri_gt = jnp.where(r_i > c_i, 0.0, NEG)

    off = pl.multiple_of((nqb - 1 - qi) * (Q // CMP_STRIDE), SUBLANES)
    bias_c = tb_ref[pl.ds(off, nc), :]
    scm = _dot_nt(kc_ref[...], q) + tile_cols(bias_c)
    m_c = jnp.max(scm, axis=0, keepdims=True)
    p_c = jnp.exp2(scm - m_c)
    l_c = jnp.sum(p_c, axis=0, keepdims=True)
    p_c = p_c * jnp.where(tok >= CMP_BLOCK - 1, 1.0 / l_c, 0.0)
    o_c = _dot(vct_ref[...], p_c.astype(BF16))

    psum = p_c[:, 0:Q]
    for h in range(1, hpg):
        psum = psum + p_c[:, h * Q:(h + 1) * Q]
    upb = SLC_BLOCK // CMP_STRIDE
    nb = nc // upb
    psum_scr[...] = psum
    rows = [psum_scr[pl.ds(r, nb, stride=upb), :] for r in range(upb)]
    inner = rows[0]
    for r in range(1, upb - 1):
        inner = inner + rows[r]
    edge = rows[upb - 1]
    first_row = lax.broadcasted_iota(jnp.int32, (nb, Q), 0) == 0
    prev_edge = jnp.where(first_row, 0.0, pltpu.roll(edge, 1, 0))
    imp = (prev_edge + edge) + 2.0 * inner
    if nbp > nb:
        imp = jnp.concatenate([imp, jnp.zeros((nbp - nb, Q), F32)], axis=0)

    nch = WINDOW // Q + 1
    tri_d = tile_cols(tri_le)
    first = qi - (nch - 1)
    idx = [jnp.maximum(first + c, 0) for c in range(nch)]
    before = [jnp.where(first + c >= 0, 0.0, NEG) for c in range(nch)]
    sw = _dot_nt(jnp.concatenate([kwd_ref[i] for i in idx], axis=0), q)
    sw = jnp.concatenate(
        [sw[0:Q] + (tile_cols(tri_gt) + before[0])]
        + [sw[c * Q:(c + 1) * Q] + before[c] for c in range(1, nch - 1)]
        + [sw[(nch - 1) * Q:nch * Q] + tri_d], axis=0)
    m_w = jnp.max(sw, axis=0, keepdims=True)
    p_w = jnp.exp2(sw - m_w).astype(BF16)
    acc_w = jnp.zeros((V_ROWS, ncol), F32)
    for c in range(nch):
        acc_w = acc_w + _dot(vwd_ref[idx[c]], p_w[c * Q:(c + 1) * Q, :])
    o_w = acc_w[0:dh, :] * (1.0 / acc_w[dh:dh + 1, :])
    o_cw = gates[0] * o_c + gates[2] * o_w

    s_d = _dot_nt(ksd_ref[...], q) + tri_d
    m_d = jnp.max(s_d, axis=0, keepdims=True)

    blk = lax.broadcasted_iota(jnp.int32, (nbp, Q), 0)
    t1 = q0 + lax.broadcasted_iota(jnp.int32, (nbp, Q), 1)
    cur = lax.shift_right_logical(t1, SLC_BLOCK.bit_length() - 1)
    forced = (blk == 0) | (blk == cur) | (blk == cur - 1)
    val = jnp.where(blk * SLC_BLOCK <= t1, jnp.where(forced, imp + FORCE_BONUS, imp), NEG)
    blk_f = blk.astype(F32)
    for _ in range(n_sel):
        mx = jnp.max(val, axis=0, keepdims=True)
        idx = jnp.min(jnp.where(val == mx, blk_f, float(nbp)), axis=0, keepdims=True)
        val = jnp.where(blk_f == idx, -jnp.inf, val)
    bias = jnp.where((val == -jnp.inf) & (blk < qi * (Q // SLC_BLOCK)), 0.0, NEG)
    bias_t = tile_cols(bias).T.astype(BF16)
    qa = jnp.concatenate([bias_t, q], axis=1)

    n_trip = q0 // kt + 1

    def score_trip(g, m):
        s = _dot_nt(ks_ref[g], qa)
        s_scr[g] = s
        return jnp.maximum(m, jnp.max(s, axis=0, keepdims=True))

    def paired(trip, carry):
        def quad(i, c):
            for u in range(4):
                c = trip(4 * i + u, c)
            return c

        carry = lax.fori_loop(0, n_trip // 4, quad, carry)
        rem = n_trip % 4
        base = n_trip - rem
        carry = lax.cond(rem >= 2, lambda c: trip(base + 1, trip(base, c)), lambda c: c, carry)
        return lax.cond(rem % 2 == 1, lambda c: trip(n_trip - 1, c), lambda c: c, carry)

    m_s = paired(score_trip, m_d)

    def value_trip(g, acc):
        p = jnp.exp2(s_scr[g] - m_s).astype(BF16)
        return acc + _dot(vst_ref[g], p)

    acc_s = _dot(vsd_ref[...], jnp.exp2(s_d - m_s).astype(BF16))
    acc_s = paired(value_trip, acc_s)
    o_s = acc_s[0:dh, :] * (1.0 / acc_s[dh:dh + 1, :])

    o_t = o_cw + gates[1] * o_s
    per = LANES // dh
    outs = []
    for hp in range(hpg // per):
        stack = jnp.concatenate([o_t[:, (hp * per + k) * Q:(hp * per + k + 1) * Q]
                                 for k in range(per)], axis=0)
        outs.append(stack.T)
    o_ref[...] = jnp.concatenate(outs, axis=1).astype(o_ref.dtype)


def _nsa(proj3, q_col, gate_col, gate_lane, hpg, cn, ct, ks, vst, ksd, vsd, kwd, vwd, tb, n_sel):
    b, s, _ = proj3.shape
    g, nqb, dh = ks.shape[1], ksd.shape[2], NSA_HEAD_DIM
    nc = cn.shape[3]
    nt, kt, ka = ks.shape[2], ks.shape[3], ks.shape[4]
    ncol = hpg * Q_BLOCK
    qw = hpg * dh
    vr = vst.shape[3]
    once = pl.Buffered(1)
    return pl.pallas_call(
        functools.partial(_nsa_kernel, n_sel=n_sel, nqb=nqb, gate_lane=gate_lane),
        grid=(b, g, nqb),
        in_specs=[
            pl.BlockSpec((None, Q_BLOCK, qw), lambda bi, gi, qi: (bi, qi, q_col // qw + gi)),
            pl.BlockSpec((None, Q_BLOCK, LANES), lambda bi, gi, qi: (bi, qi, gate_col // LANES + gi)),
            pl.BlockSpec((None, None, None, nc, dh), lambda bi, gi, qi: (0, bi, gi, 0, 0)),
            pl.BlockSpec((None, None, None, dh, nc), lambda bi, gi, qi: (1, bi, gi, 0, 0)),
            pl.BlockSpec((None, None, nt, kt, ka), lambda bi, gi, qi: (bi, gi, 0, 0, 0),
                         pipeline_mode=once),
            pl.BlockSpec((None, None, nt, vr, kt), lambda bi, gi, qi: (bi, gi, 0, 0, 0),
                         pipeline_mode=once),
            pl.BlockSpec((None, None, None, Q_BLOCK, dh), lambda bi, gi, qi: (bi, gi, qi, 0, 0)),
            pl.BlockSpec((None, None, None, vr, Q_BLOCK), lambda bi, gi, qi: (bi, gi, qi, 0, 0)),
            pl.BlockSpec((None, None, nqb, Q_BLOCK, dh), lambda bi, gi, qi: (bi, gi, 0, 0, 0),
                         pipeline_mode=once),
            pl.BlockSpec((None, None, nqb, vr, Q_BLOCK), lambda bi, gi, qi: (bi, gi, 0, 0, 0),
                         pipeline_mode=once),
            pl.BlockSpec(tb.shape, lambda bi, gi, qi: (0, 0)),
        ],
        out_specs=pl.BlockSpec((None, Q_BLOCK, qw), lambda bi, gi, qi: (bi, qi, gi)),
        out_shape=jax.ShapeDtypeStruct((b, s, g * qw), BF16),
        scratch_shapes=[pltpu.VMEM((nt, kt, ncol), F32), pltpu.VMEM((nc, Q_BLOCK), F32)],
        compiler_params=_cparams(("parallel", "parallel", "arbitrary")),
        name="nsa",
    )(proj3, proj3, cn, ct, ks, vst, ksd, vsd, kwd, vwd, tb)


def _kvprep_kernel(pc_ref, ps_ref, pw_ref, craw_ref, ks_ref, vst_ref, ksd_ref, vsd_ref,
                   kwd_ref, vwd_ref):
    dh = NSA_HEAD_DIM
    Q = Q_BLOCK
    kt = ps_ref.shape[0]
    nbp = ks_ref.shape[1] - dh
    j = pl.program_id(2)
    for p in range(CMP_STRIDE):
        rows = pc_ref[pl.ds(p, kt // CMP_STRIDE, stride=CMP_STRIDE), :]
        craw_ref[0, :, p * dh:(p + 1) * dh] = rows[:, 0:dh]
        craw_ref[1, :, p * dh:(p + 1) * dh] = rows[:, dh:2 * dh]

    tail_row = lax.broadcasted_iota(jnp.int32, (V_ROWS - dh, kt), 0)
    tail = jnp.where(tail_row == 0, 1.0, 0.0).astype(BF16)

    def k_and_vt(p_ref):
        p = p_ref[...]
        k = p[:, 0:dh].astype(BF16)
        vt = jnp.concatenate([p.T[dh:2 * dh, :].astype(BF16), tail], axis=0)
        return k, vt

    k_s, vt_s = k_and_vt(ps_ref)
    key = lax.broadcasted_iota(jnp.int32, (kt, nbp), 0)
    bcol = lax.broadcasted_iota(jnp.int32, (kt, nbp), 1)
    blk_of_key = j * (kt // SLC_BLOCK) + lax.shift_right_logical(key, SLC_BLOCK.bit_length() - 1)
    ks_ref[:, 0:nbp] = jnp.where(bcol == blk_of_key, 1.0, 0.0).astype(BF16)
    ks_ref[:, nbp:nbp + dh] = k_s
    vst_ref[...] = vt_s
    k_w, vt_w = k_and_vt(pw_ref)
    for c in range(kt // Q):
        ksd_ref[c] = k_s[c * Q:(c + 1) * Q, :]
        vsd_ref[c] = vt_s[:, c * Q:(c + 1) * Q]
        kwd_ref[c] = k_w[c * Q:(c + 1) * Q, :]
        vwd_ref[c] = vt_w[:, c * Q:(c + 1) * Q]


def _kvprep(proj3, kv_col, g, kt, nbp):
    b, s, _ = proj3.shape
    dh = NSA_HEAD_DIM
    nt, nqb, cpt = s // kt, s // Q_BLOCK, kt // Q_BLOCK
    pair = 2 * dh
    base = kv_col // pair

    def pspec(branch):
        return pl.BlockSpec((None, kt, pair), lambda bi, gi, j: (bi, j, base + 3 * gi + branch))

    chunk_k = pl.BlockSpec((None, None, cpt, Q_BLOCK, dh), lambda bi, gi, j: (bi, gi, j, 0, 0))
    chunk_v = pl.BlockSpec((None, None, cpt, V_ROWS, Q_BLOCK), lambda bi, gi, j: (bi, gi, j, 0, 0))
    return pl.pallas_call(
        _kvprep_kernel,
        grid=(b, g, nt),
        in_specs=[pspec(0), pspec(1), pspec(2)],
        out_specs=[
            pl.BlockSpec((2, None, None, kt // CMP_STRIDE, CMP_STRIDE * dh),
                         lambda bi, gi, j: (0, bi, gi, j, 0)),
            pl.BlockSpec((None, None, None, kt, nbp + dh), lambda bi, gi, j: (bi, gi, j, 0, 0)),
            pl.BlockSpec((None, None, None, V_ROWS, kt), lambda bi, gi, j: (bi, gi, j, 0, 0)),
            chunk_k, chunk_v, chunk_k, chunk_v,
        ],
        out_shape=[
            jax.ShapeDtypeStruct((2, b, g, s // CMP_STRIDE, CMP_STRIDE * dh), F32),
            jax.ShapeDtypeStruct((b, g, nt, kt, nbp + dh), BF16),
            jax.ShapeDtypeStruct((b, g, nt, V_ROWS, kt), BF16),
            jax.ShapeDtypeStruct((b, g, nqb, Q_BLOCK, dh), BF16),
            jax.ShapeDtypeStruct((b, g, nqb, V_ROWS, Q_BLOCK), BF16),
            jax.ShapeDtypeStruct((b, g, nqb, Q_BLOCK, dh), BF16),
            jax.ShapeDtypeStruct((b, g, nqb, V_ROWS, Q_BLOCK), BF16),
        ],
        compiler_params=_cparams(("parallel", "parallel", "parallel")),
        name="kvprep",
    )(proj3, proj3, proj3)


def _outproj_kernel(h_ref, ys_ref, yn_ref, yc_ref, w_ref, nw_ref, o_ref):
    w0 = ys_ref.shape[1]
    w1 = w0 + yn_ref.shape[1]
    w2 = w1 + yc_ref.shape[1]
    m = (_dot(ys_ref[...], w_ref[0:w0, :]) + _dot(yn_ref[...], w_ref[w0:w1, :])
         + _dot(yc_ref[...], w_ref[w1:w2, :]))
    o_ref[...] = h_ref[...] + _rms(m, nw_ref[...])


def _outproj(h, ys, yn, yc, w, nw, layer):
    n, d = h.shape
    dm = w.shape[1]
    return pl.pallas_call(
        _outproj_kernel,
        grid=(n // TOKEN_TILE,),
        in_specs=[
            pl.BlockSpec((TOKEN_TILE, d), lambda i: (i, 0)),
            pl.BlockSpec((TOKEN_TILE, ys.shape[1]), lambda i: (i, 0)),
            pl.BlockSpec((TOKEN_TILE, yn.shape[1]), lambda i: (i, 0)),
            pl.BlockSpec((TOKEN_TILE, yc.shape[1]), lambda i: (i, 0)),
            pl.BlockSpec((None, dm, d), lambda i: (layer, 0, 0)),
            pl.BlockSpec((None, 1, d), lambda i: (layer, 0, 0)),
        ],
        out_specs=pl.BlockSpec((TOKEN_TILE, d), lambda i: (i, 0)),
        out_shape=jax.ShapeDtypeStruct((n, d), F32),
        compiler_params=_cparams(("parallel",)),
        name="outproj",
    )(h, ys, yn, yc, w, nw)


def _nsa_constants(nc, nqb):
    upq = Q_BLOCK // CMP_STRIDE
    r = np.arange(nc + upq * (nqb - 1))[:, None]
    tl = np.arange(Q_BLOCK)[None, :]
    tb = np.where(CMP_STRIDE * r + (CMP_BLOCK - 1) - Q_BLOCK * (nqb - 1) <= tl, 0.0, NEG)
    return jnp.asarray(tb, F32)


def kernel(x, ffn1_pre_norm, ffn1_w_gu, ffn1_w_down, ffn1_post_norm, mix_pre_norm, w_in,
           ssd_conv_w, ssd_conv_b, ssd_dt_bias, ssd_a_log, ssd_d, ssd_norm_w,
           nsa_k_pe, nsa_k_w1, nsa_k_b1, nsa_k_w2, nsa_v_pe, nsa_v_w1, nsa_v_b1, nsa_v_w2,
           cnv_dw_w, cnv_dw_b, cnv_ln_w, cnv_ln_b, w_out, mix_post_norm,
           ffn2_pre_norm, ffn2_w_gu, ffn2_w_down, ffn2_post_norm):
    bsz, s, d = x.shape
    depth = w_in.shape[0]
    n = bsz * s
    n_heads = ssd_dt_bias.shape[1]
    ssd_w = n_heads * SSD_HEAD_DIM
    bcw = 2 * SSD_GROUPS * SSD_STATE
    cch = cnv_dw_w.shape[2]
    d_in = w_in.shape[2]
    kvw = NSA_KV_GROUPS * NSA_HEAD_DIM
    n_gate = d_in - (2 * ssd_w + bcw + n_heads + 6 * kvw + 2 * cch)
    nsa_heads = n_gate // (NSA_HEAD_DIM + 3)
    nsa_w = nsa_heads * NSA_HEAD_DIM
    G = NSA_KV_GROUPS
    hpg = nsa_heads // G
    dh = NSA_HEAD_DIM

    o_z, o_xbc, o_dt = 0, ssd_w, ssd_w + ssd_w + bcw
    o_q = o_dt + n_heads
    o_kv = o_q + nsa_w
    o_gl = o_kv + 6 * kvw
    o_glu = o_gl + 3 * nsa_heads
    gl_w = 3 * hpg
    src = [("z", o_z, ssd_w), ("glu", o_glu, 2 * cch), ("xs", o_xbc, ssd_w),
           ("bc", o_xbc + ssd_w, bcw), ("q", o_q, nsa_w)]
    cols, pos, pieces = {}, 0, []
    for name, o, w in src:
        cols[name] = pos
        pieces.append(w_in[:, :, o:o + w].astype(BF16))
        pos += w
    kv_cols = w_in[:, :, o_kv:o_kv + 6 * kvw].astype(BF16).reshape(depth, d, 6, G, dh)
    pieces.append(kv_cols.transpose(0, 1, 3, 2, 4).reshape(depth, d, 6 * kvw))
    cols["kv"] = pos
    pos += 6 * kvw
    zeros = lambda w: jnp.zeros((depth, d, w), BF16)
    cols["dt"] = cols["gate"] = pos
    assert pos % LANES == 0 and n_heads + gl_w <= LANES
    for gi in range(G):
        head = w_in[:, :, o_dt:o_dt + n_heads].astype(BF16) if gi == 0 else zeros(n_heads)
        pieces += [head, w_in[:, :, o_gl + gi * gl_w:o_gl + (gi + 1) * gl_w].astype(BF16),
                   zeros(LANES - n_heads - gl_w)]
        pos += LANES
    col_tile = 10 * LANES
    total = -(-pos // col_tile) * col_tile
    if total > pos:
        pieces.append(zeros(total - pos))
    w_in_r = jnp.concatenate(pieces, axis=2)
    assert cols["z"] % ssd_w == 0 and cols["xs"] % ssd_w == 0 and cols["bc"] % bcw == 0
    assert cols["glu"] % (2 * cch) == 0 and cols["q"] % (hpg * dh) == 0 and cols["kv"] % (2 * dh) == 0

    r3 = lambda a: a.reshape(depth, 1, -1)
    w1_gu, w1_dn = ffn1_w_gu.astype(BF16), ffn1_w_down.astype(BF16)
    w2_gu, w2_dn = ffn2_w_gu.astype(BF16), ffn2_w_down.astype(BF16)
    w_out16 = w_out.astype(BF16)
    cwx, cwb = ssd_conv_w[:, :, :ssd_w], ssd_conv_w[:, :, ssd_w:]
    cbx, cbb = r3(ssd_conv_b[:, :ssd_w]), r3(ssd_conv_b[:, ssd_w:])
    padl = lambda a: jnp.pad(a, ((0, 0), (0, LANES - a.shape[1]))).reshape(depth, 1, LANES)
    dtb, alog = padl(ssd_dt_bias), padl(ssd_a_log)
    dsk = r3(jnp.repeat(ssd_d, SSD_HEAD_DIM, axis=1))
    e_np = np.zeros((LANES, ssd_w), np.float32)
    for hh in range(n_heads):
        e_np[hh, hh * SSD_HEAD_DIM:(hh + 1) * SSD_HEAD_DIM] = 1.0
    e_mat = jnp.asarray(e_np, dtype=BF16)
    pe = jnp.stack([nsa_k_pe, nsa_v_pe]).reshape(2, depth, 1, CMP_BLOCK * dh)
    cw1 = jnp.stack([nsa_k_w1, nsa_v_w1]).astype(BF16)
    cb1 = jnp.stack([nsa_k_b1, nsa_v_b1]).reshape(2, depth, 1, -1)
    cw2 = jnp.stack([nsa_k_w2, nsa_v_w2]).astype(BF16)
    cw2t = jnp.swapaxes(cw2, 2, 3)
    nc = s // CMP_STRIDE
    nb = s // SLC_BLOCK
    nbp = -(-nb // LANES) * LANES
    nqb = s // Q_BLOCK
    kt = min(KV_TILE, s)
    nt = s // kt
    n_sel = min(SLC_TOPN, nb)
    tb = _nsa_constants(nc, nqb)
    assert CMP_BLOCK == 2 * CMP_STRIDE and SLC_BLOCK % CMP_STRIDE == 0

    def nsa_branch(proj3, l):
        u16, ks, vst, ksd, vsd, kwd, vwd = _kvprep(proj3, cols["kv"], G, kt, nbp)
        cn, ct = _cmp(u16, pe, cw1, cb1, cw2, cw2t, l)
        return _nsa(proj3, cols["q"], cols["gate"], n_heads, hpg, cn, ct, ks, vst, ksd, vsd,
                    kwd, vwd, tb, n_sel)

    h = x.reshape(n, d)
    for l in range(depth):
        h = _ffn(h, r3(ffn1_pre_norm), w1_gu, w1_dn, r3(ffn1_post_norm), l)

        proj = _inproj(h, r3(mix_pre_norm), w_in_r, l, col_tile)
        proj3 = proj.reshape(bsz, s, total)
        y_ssd = _ssd(proj3, cols, cwx, cbx, cwb, cbb, dtb, alog, dsk, r3(ssd_norm_w), e_mat, l, n_heads)
        y_cnv = _cnv(proj3, cols["glu"], cnv_dw_w, r3(cnv_dw_b), r3(cnv_ln_w), r3(cnv_ln_b), l)
        y_nsa = nsa_branch(proj3, l)

        h = _outproj(h, y_ssd.reshape(n, ssd_w), y_nsa.reshape(n, nsa_w), y_cnv.reshape(n, cch),
                     w_out16, r3(mix_post_norm), l)
        h = _ffn(h, r3(ffn2_pre_norm), w2_gu, w2_dn, r3(ffn2_post_norm), l)
    return h.reshape(bsz, s, d)
```

```python
import functools
import math

import numpy as np
import jax
import jax.numpy as jnp
from jax import lax
from jax.experimental import pallas as pl
from jax.experimental.pallas import tpu as pltpu

F32 = jnp.float32
BF16 = jnp.bfloat16

SSD_HEAD_DIM = 64
SSD_GROUPS = 2
SSD_STATE = 128
SSD_CONV = 4
SSD_CHUNK = 128
NSA_HEAD_DIM = 64
NSA_KV_GROUPS = 2
CMP_BLOCK = 32
CMP_STRIDE = 16
SLC_BLOCK = 64
SLC_TOPN = 16
WINDOW = 512
Q_BLOCK = 128
FORCE_BONUS = 1000.0
CONV_KERNEL = 31
HALF = 0.5
EPS = 1e-6
NEG = -1e30

LANES = 128
SUBLANES = 8
BF16_ROWS = 16
VMEM_LIMIT = 56 * 1024 * 1024
TOKEN_TILE = 512
PROJ_TOKEN_TILE = 1024
FF_TILE = 512
KV_TILE = 1024
CONV_TILE = 256
CONV_HALO = 32
V_ROWS = NSA_HEAD_DIM + BF16_ROWS


def _cparams(sem):
    return pltpu.CompilerParams(dimension_semantics=sem, vmem_limit_bytes=VMEM_LIMIT)


def _rms(x, w):
    return (x * lax.rsqrt(jnp.mean(x * x, axis=-1, keepdims=True) + EPS)) * w


def _silu(x):
    return x * jax.nn.sigmoid(x)


def _dot(a, b):
    return jnp.dot(a, b, preferred_element_type=F32)


def _dot_nt(a, b):
    return lax.dot_general(a, b, (((1,), (1,)), ((), ())), preferred_element_type=F32)


def _split3(x):
    hi = x.astype(BF16)
    r1 = x - hi.astype(F32)
    mid = r1.astype(BF16)
    lo = (r1 - mid.astype(F32)).astype(BF16)
    return hi, mid, lo


def _dot_exact_lhs(a, x):
    hi, mid, lo = _split3(x)
    return _dot(a, hi) + _dot(a, mid) + _dot(a, lo)


def _ffn_kernel(h_ref, pre_ref, wg_ref, wu_ref, wd_ref, post_ref, o_ref, u_scr, acc_scr, *, nf):
    j = pl.program_id(1)
    last = nf - 1

    def step(first, final):
        if first:
            u = _rms(h_ref[...], pre_ref[...]).astype(BF16)
            u_scr[...] = u
        else:
            u = u_scr[...]
        g = _dot(u, wg_ref[...])
        v = _dot(u, wu_ref[...])
        part = _dot((_silu(g) * v).astype(BF16), wd_ref[...])
        acc = part if first else acc_scr[...] + part
        if final:
            o_ref[...] = h_ref[...] + HALF * _rms(acc, post_ref[...])
        else:
            acc_scr[...] = acc

    if nf == 1:
        step(True, True)
    else:
        pl.when(j == 0)(lambda: step(True, False))
        pl.when((j > 0) & (j < last))(lambda: step(False, False))
        pl.when(j == last)(lambda: step(False, True))


def _ffn(h, pre_w, w_gu, w_down, post_w, layer):
    n, d = h.shape
    f = w_down.shape[1]
    nf = f // FF_TILE
    return pl.pallas_call(
        functools.partial(_ffn_kernel, nf=nf),
        grid=(n // TOKEN_TILE, nf),
        in_specs=[
            pl.BlockSpec((TOKEN_TILE, d), lambda i, j: (i, 0)),
            pl.BlockSpec((None, 1, d), lambda i, j: (layer, 0, 0)),
            pl.BlockSpec((None, d, FF_TILE), lambda i, j: (layer, 0, j)),
            pl.BlockSpec((None, d, FF_TILE), lambda i, j: (layer, 0, j + nf)),
            pl.BlockSpec((None, FF_TILE, d), lambda i, j: (layer, j, 0)),
            pl.BlockSpec((None, 1, d), lambda i, j: (layer, 0, 0)),
        ],
        out_specs=pl.BlockSpec((TOKEN_TILE, d), lambda i, j: (i, 0)),
        out_shape=jax.ShapeDtypeStruct((n, d), F32),
        scratch_shapes=[pltpu.VMEM((TOKEN_TILE, d), BF16), pltpu.VMEM((TOKEN_TILE, d), F32)],
        compiler_params=_cparams(("parallel", "arbitrary")),
        name="ffn",
    )(h, pre_w, w_gu, w_gu, w_down, post_w)


def _inproj_kernel(h_ref, nw_ref, w_ref, o_ref, u_scr):
    @pl.when(pl.program_id(1) == 0)
    def _():
        u = _rms(h_ref[...], nw_ref[...]).astype(BF16)
        u_scr[...] = u
        o_ref[...] = _dot(u, w_ref[...])

    @pl.when(pl.program_id(1) > 0)
    def _():
        o_ref[...] = _dot(u_scr[...], w_ref[...])


def _inproj(h, nw, w, layer, col_tile):
    n, d = h.shape
    cols = w.shape[2]
    return pl.pallas_call(
        _inproj_kernel,
        grid=(n // PROJ_TOKEN_TILE, cols // col_tile),
        in_specs=[
            pl.BlockSpec((PROJ_TOKEN_TILE, d), lambda i, j: (i, 0)),
            pl.BlockSpec((None, 1, d), lambda i, j: (layer, 0, 0)),
            pl.BlockSpec((None, d, col_tile), lambda i, j: (layer, 0, j)),
        ],
        out_specs=pl.BlockSpec((PROJ_TOKEN_TILE, col_tile), lambda i, j: (i, j)),
        out_shape=jax.ShapeDtypeStruct((n, cols), F32),
        scratch_shapes=[pltpu.VMEM((PROJ_TOKEN_TILE, d), BF16)],
        compiler_params=_cparams(("parallel", "arbitrary")),
        name="inproj",
    )(h, nw, w)


def _ssd_kernel(z_ref, xs_ref, bc_ref, dt_ref, cwx_ref, cbx_ref, cwb_ref, cbb_ref, dtb_ref,
                alog_ref, dsk_ref, nw_ref, e_ref, o_ref, xpad, bpad, hst, *, n_heads):
    L = SSD_CHUNK
    N = SSD_STATE
    P = SSD_HEAD_DIM
    G = SSD_GROUPS
    hg = n_heads // G
    gw = hg * P
    halo = SUBLANES

    @pl.when(pl.program_id(1) == 0)
    def _():
        xpad[0:halo, :] = jnp.zeros((halo, xpad.shape[1]), F32)
        bpad[0:halo, :] = jnp.zeros((halo, bpad.shape[1]), F32)
        hst[...] = jnp.zeros_like(hst)

    xpad[halo:halo + L, :] = xs_ref[...]
    bpad[halo:halo + L, :] = bc_ref[...]

    def conv_silu(pad, w_ref, b_ref):
        full = pad[...]
        acc = b_ref[...]
        for k in range(SSD_CONV):
            back = SSD_CONV - 1 - k
            src = full if back == 0 else pltpu.roll(full, back, 0)
            acc = acc + w_ref[k:k + 1, :] * src[halo:halo + L, :]
        return _silu(acc)

    x = conv_silu(xpad, cwx_ref, cbx_ref)
    bcv = conv_silu(bpad, cwb_ref, cbb_ref)
    xpad[0:halo, :] = xpad[L:L + halo, :]
    bpad[0:halo, :] = bpad[L:L + halo, :]

    lane = lax.broadcasted_iota(jnp.int32, (1, LANES), 1)
    dtv = dt_ref[...] + dtb_ref[...]
    dt = jnp.maximum(dtv, 0.0) + jnp.log(1.0 + jnp.exp(-jnp.abs(dtv)))
    dt = jnp.where(lane < n_heads, dt, 0.0)
    a = -jnp.exp(alog_ref[...])
    da = dt * a
    row = lax.broadcasted_iota(jnp.int32, (L, L), 0)
    col = lax.broadcasted_iota(jnp.int32, (L, L), 1)
    tril = row >= col
    cs = _dot_exact_lhs(tril.astype(BF16), da)
    cs_t = cs.T
    dt_t = dt.T
    cs_last = cs[L - 1:L, :]
    e = e_ref[...]
    ecs = _dot(jnp.exp(cs).astype(BF16), e)
    dend = _dot((jnp.exp(cs_last - cs) * dt).astype(BF16), e)
    lane_p = lax.broadcasted_iota(jnp.int32, (L, 2 * P), 1)

    ys = []
    for g in range(G):
        bm = bcv[:, g * N:(g + 1) * N]
        cm = bcv[:, G * N + g * N:G * N + (g + 1) * N]
        bm16 = bm.astype(BF16)
        cm16 = cm.astype(BF16)
        cb = _dot_nt(cm16, bm16)
        xg = x[:, g * gw:(g + 1) * gw]
        yd = []
        for hp in range(hg // 2):
            ws = []
            for hh in range(2):
                h = g * hg + hp * 2 + hh
                seg = cs[:, h:h + 1] - cs_t[h:h + 1, :]
                dec = jnp.exp(jnp.where(tril, seg, -jnp.inf))
                ws.append(cb * dec * dt_t[h:h + 1, :])
            wcat = jnp.concatenate(ws, axis=1).astype(BF16)
            xp = xg[:, hp * 2 * P:(hp + 1) * 2 * P]
            xbd = jnp.concatenate([jnp.where(lane_p < P, xp, 0.0),
                                   jnp.where(lane_p >= P, xp, 0.0)], axis=0).astype(BF16)
            yd.append(_dot(wcat, xbd))
        y_diag = jnp.concatenate(yd, axis=1)
        hprev = hst[g]
        y_off = _dot(cm16, hprev.astype(BF16)) * ecs[:, g * gw:(g + 1) * gw]
        st = _dot(bm.T.astype(BF16), (xg * dend[:, g * gw:(g + 1) * gw]).astype(BF16))
        hst[g] = hprev * ecs[L - 1:L, g * gw:(g + 1) * gw] + st
        y = y_diag + y_off + dsk_ref[:, g * gw:(g + 1) * gw] * xg
        y = y * _silu(z_ref[:, g * gw:(g + 1) * gw])
        y = y * lax.rsqrt(jnp.mean(y * y, axis=-1, keepdims=True) + EPS)
        ys.append(y * nw_ref[:, g * gw:(g + 1) * gw])
    o_ref[...] = jnp.concatenate(ys, axis=1).astype(o_ref.dtype)


def _ssd(proj3, cols, cwx, cbx, cwb, cbb, dtb, alog, dsk, nw, e, layer, n_heads):
    b, s, _ = proj3.shape
    width = n_heads * SSD_HEAD_DIM
    bcw = 2 * SSD_GROUPS * SSD_STATE
    L = SSD_CHUNK
    gw = width // SSD_GROUPS

    def wspec(c):
        return pl.BlockSpec((None, 1, c), lambda bi, ci: (layer, 0, 0))

    return pl.pallas_call(
        functools.partial(_ssd_kernel, n_heads=n_heads),
        grid=(b, s // L),
        in_specs=[
            pl.BlockSpec((None, L, width), lambda bi, ci: (bi, ci, cols["z"] // width)),
            pl.BlockSpec((None, L, width), lambda bi, ci: (bi, ci, cols["xs"] // width)),
            pl.BlockSpec((None, L, bcw), lambda bi, ci: (bi, ci, cols["bc"] // bcw)),
            pl.BlockSpec((None, L, LANES), lambda bi, ci: (bi, ci, cols["dt"] // LANES)),
            pl.BlockSpec((None, SSD_CONV, width), lambda bi, ci: (layer, 0, 0)),
            wspec(width),
            pl.BlockSpec((None, SSD_CONV, bcw), lambda bi, ci: (layer, 0, 0)),
            wspec(bcw),
            wspec(LANES),
            wspec(LANES),
            wspec(width),
            wspec(width),
            pl.BlockSpec((LANES, width), lambda bi, ci: (0, 0)),
        ],
        out_specs=pl.BlockSpec((None, L, width), lambda bi, ci: (bi, ci, 0)),
        out_shape=jax.ShapeDtypeStruct((b, s, width), BF16),
        scratch_shapes=[pltpu.VMEM((L + SUBLANES, width), F32),
                        pltpu.VMEM((L + SUBLANES, bcw), F32),
                        pltpu.VMEM((SSD_GROUPS, SSD_STATE, gw), F32)],
        compiler_params=_cparams(("parallel", "arbitrary")),
        name="ssd",
    )(proj3, proj3, proj3, proj3, cwx, cbx, cwb, cbb, dtb, alog, dsk, nw, e)


def _cnv_kernel(glu_ref, w_ref, b_ref, lw_ref, lb_ref, o_ref, pad, shifted):
    T = CONV_TILE
    c = pad.shape[1]

    @pl.when(pl.program_id(1) == 0)
    def _():
        pad[0:CONV_HALO, :] = jnp.zeros((CONV_HALO, c), F32)

    pad[CONV_HALO:CONV_HALO + T, :] = glu_ref[:, 0:c] * jax.nn.sigmoid(glu_ref[:, c:2 * c])
    span = T + CONV_HALO - SUBLANES
    for r in range(1, SUBLANES):
        shifted[r - 1] = pad[r:r + span, :]
    acc = b_ref[...]
    for k in range(CONV_KERNEL):
        off = CONV_HALO - (CONV_KERNEL - 1) + k
        r, base = off % SUBLANES, off - off % SUBLANES
        src = pad[base:base + T, :] if r == 0 else shifted[r - 1, base:base + T, :]
        acc = acc + w_ref[k:k + 1, :] * src
    pad[0:CONV_HALO, :] = pad[T:T + CONV_HALO, :]
    mu = jnp.mean(acc, axis=-1, keepdims=True)
    cen = acc - mu
    var = jnp.mean(cen * cen, axis=-1, keepdims=True)
    v = cen * lax.rsqrt(var + EPS) * lw_ref[...] + lb_ref[...]
    o_ref[...] = _silu(v).astype(o_ref.dtype)


def _cnv(proj3, col, w, bias, lw, lb, layer):
    b, s, _ = proj3.shape
    c = w.shape[2]
    T = CONV_TILE
    return pl.pallas_call(
        _cnv_kernel,
        grid=(b, s // T),
        in_specs=[
            pl.BlockSpec((None, T, 2 * c), lambda bi, ti: (bi, ti, col // (2 * c))),
            pl.BlockSpec((None, CONV_KERNEL, c), lambda bi, ti: (layer, 0, 0)),
            pl.BlockSpec((None, 1, c), lambda bi, ti: (layer, 0, 0)),
            pl.BlockSpec((None, 1, c), lambda bi, ti: (layer, 0, 0)),
            pl.BlockSpec((None, 1, c), lambda bi, ti: (layer, 0, 0)),
        ],
        out_specs=pl.BlockSpec((None, T, c), lambda bi, ti: (bi, ti, 0)),
        out_shape=jax.ShapeDtypeStruct((b, s, c), BF16),
        scratch_shapes=[pltpu.VMEM((T + CONV_HALO, c), F32),
                        pltpu.VMEM((SUBLANES - 1, T + CONV_HALO - SUBLANES, c), F32)],
        compiler_params=_cparams(("parallel", "arbitrary")),
        name="cnv",
    )(proj3, w, bias, lw, lb)


def _cmp_kernel(u_ref, pe_ref, w1_ref, b1_ref, w2_ref, w2t_ref, on_ref, ot_ref):
    u = u_ref[...]
    half = u.shape[1]
    nc = u.shape[0]
    a0 = (u + pe_ref[:, 0:half]).astype(BF16)
    a1 = (u + pe_ref[:, half:2 * half]).astype(BF16)
    h0 = _dot(a0, w1_ref[0:half, :])
    h1 = _dot(a1, w1_ref[half:2 * half, :])
    pre = h0 + pltpu.roll(h1, nc - 1, 0) + b1_ref[...]
    act = _silu(pre).astype(BF16)
    on_ref[...] = _dot(act, w2_ref[...]).astype(on_ref.dtype)
    ot_ref[...] = _dot_nt(w2t_ref[...], act).astype(ot_ref.dtype)


def _cmp(u16, pe, w1, b1, w2, w2t, layer):
    _, b, g, nc, uw = u16.shape
    hid = w1.shape[3]
    dh = w2.shape[3]
    return pl.pallas_call(
        _cmp_kernel,
        grid=(2, b, g),
        in_specs=[
            pl.BlockSpec((None, None, None, nc, uw), lambda k, bi, gi: (k, bi, gi, 0, 0)),
            pl.BlockSpec((None, None, 1, 2 * uw), lambda k, bi, gi: (k, layer, 0, 0)),
            pl.BlockSpec((None, None, 2 * uw, hid), lambda k, bi, gi: (k, layer, 0, 0)),
            pl.BlockSpec((None, None, 1, hid), lambda k, bi, gi: (k, layer, 0, 0)),
            pl.BlockSpec((None, None, hid, dh), lambda k, bi, gi: (k, layer, 0, 0)),
            pl.BlockSpec((None, None, dh, hid), lambda k, bi, gi: (k, layer, 0, 0)),
        ],
        out_specs=[
            pl.BlockSpec((None, None, None, nc, dh), lambda k, bi, gi: (k, bi, gi, 0, 0)),
            pl.BlockSpec((None, None, None, dh, nc), lambda k, bi, gi: (k, bi, gi, 0, 0)),
        ],
        out_shape=[jax.ShapeDtypeStruct((2, b, g, nc, dh), BF16),
                   jax.ShapeDtypeStruct((2, b, g, dh, nc), BF16)],
        compiler_params=_cparams(("parallel", "parallel", "parallel")),
        name="cmp",
    )(u16, pe, w1, b1, w2, w2t)


def _nsa_kernel(q_ref, gl_ref, kc_ref, vct_ref, ks_ref, vst_ref, ksd_ref, vsd_ref, kwd_ref,
                vwd_ref, tb_ref, o_ref, s_scr, psum_scr, *, n_sel, nqb, gate_lane):
    Q = Q_BLOCK
    dh = NSA_HEAD_DIM
    hpg = q_ref.shape[1] // dh
    ncol = hpg * Q
    nc = kc_ref.shape[0]
    kt = ks_ref.shape[1]
    nbp = ks_ref.shape[2] - dh
    qi = pl.program_id(2)
    q0 = qi * Q

    qt = q_ref[...] * (dh ** -0.5 * math.log2(math.e))
    q = jnp.concatenate([qt[:, h * dh:(h + 1) * dh] for h in range(hpg)], axis=0).astype(BF16)
    gl_t = gl_ref[...].T
    gates = [jax.nn.sigmoid(jnp.concatenate(
        [gl_t[gate_lane + 3 * h + k:gate_lane + 3 * h + k + 1, :] for h in range(hpg)], axis=1))
        for k in range(3)]
    lane = lax.broadcasted_iota(jnp.int32, (1, ncol), 1)
    tok = q0 + (lane & (Q - 1))

    def tile_cols(a):
        return jnp.concatenate([a] * hpg, axis=1)

    r_i = lax.broadcasted_iota(jnp.int32, (Q, Q), 0)
    c_i = lax.broadcasted_iota(jnp.int32, (Q, Q), 1)
    tri_le = jnp.where(r_i <= c_i, 0.0, NEG)
    tri_gt = jnp.where(r_i > c_i, 0.0, NEG)

    off = pl.multiple_of((nqb - 1 - qi) * (Q // CMP_STRIDE), SUBLANES)
    bias_c = tb_ref[pl.ds(off, nc), :]
    scm = _dot_nt(kc_ref[...], q) + tile_cols(bias_c)
    m_c = jnp.max(scm, axis=0, keepdims=True)
    p_c = jnp.exp2(scm - m_c)
    l_c = jnp.sum(p_c, axis=0, keepdims=True)
    p_c = p_c * jnp.where(tok >= CMP_BLOCK - 1, 1.0 / l_c, 0.0)
    o_c = _dot(vct_ref[...], p_c.astype(BF16))

    psum = p_c[:, 0:Q]
    for h in range(1, hpg):
        psum = psum + p_c[:, h * Q:(h + 1) * Q]
    upb = SLC_BLOCK // CMP_STRIDE
    nb = nc // upb
    psum_scr[...] = psum
    rows = [psum_scr[pl.ds(r, nb, stride=upb), :] for r in range(upb)]
    inner = rows[0]
    for r in range(1, upb - 1):
        inner = inner + rows[r]
    edge = rows[upb - 1]
    first_row = lax.broadcasted_iota(jnp.int32, (nb, Q), 0) == 0
    prev_edge = jnp.where(first_row, 0.0, pltpu.roll(edge, 1, 0))
    imp = (prev_edge + edge) + 2.0 * inner
    if nbp > nb:
        imp = jnp.concatenate([imp, jnp.zeros((nbp - nb, Q), F32)], axis=0)

    nch = WINDOW // Q + 1
    tri_d = tile_cols(tri_le)
    first = qi - (nch - 1)
    idx = [jnp.maximum(first + c, 0) for c in range(nch)]
    before = [jnp.where(first + c >= 0, 0.0, NEG) for c in range(nch)]
    sw = _dot_nt(jnp.concatenate([kwd_ref[i] for i in idx], axis=0), q)
    sw = jnp.concatenate(
        [sw[0:Q] + (tile_cols(tri_gt) + before[0])]
        + [sw[c * Q:(c + 1) * Q] + before[c] for c in range(1, nch - 1)]
        + [sw[(nch - 1) * Q:nch * Q] + tri_d], axis=0)
    m_w = jnp.max(sw, axis=0, keepdims=True)
    p_w = jnp.exp2(sw - m_w).astype(BF16)
    acc_w = jnp.zeros((V_ROWS, ncol), F32)
    for c in range(nch):
        acc_w = acc_w + _dot(vwd_ref[idx[c]], p_w[c * Q:(c + 1) * Q, :])
    o_w = acc_w[0:dh, :] * (1.0 / acc_w[dh:dh + 1, :])
    o_cw = gates[0] * o_c + gates[2] * o_w

    s_d = _dot_nt(ksd_ref[...], q) + tri_d
    m_d = jnp.max(s_d, axis=0, keepdims=True)

    blk = lax.broadcasted_iota(jnp.int32, (nbp, Q), 0)
    t1 = q0 + lax.broadcasted_iota(jnp.int32, (nbp, Q), 1)
    cur = lax.shift_right_logical(t1, SLC_BLOCK.bit_length() - 1)
    forced = (blk == 0) | (blk == cur) | (blk == cur - 1)
    val = jnp.where(blk * SLC_BLOCK <= t1, jnp.where(forced, imp + FORCE_BONUS, imp), NEG)
    blk_f = blk.astype(F32)
    for _ in range(n_sel):
        mx = jnp.max(val, axis=0, keepdims=True)
        idx = jnp.min(jnp.where(val == mx, blk_f, float(nbp)), axis=0, keepdims=True)
        val = jnp.where(blk_f == idx, -jnp.inf, val)
    bias = jnp.where((val == -jnp.inf) & (blk < qi * (Q // SLC_BLOCK)), 0.0, NEG)
    bias_t = tile_cols(bias).T.astype(BF16)
    qa = jnp.concatenate([bias_t, q], axis=1)

    n_trip = q0 // kt + 1

    def score_trip(g, m):
        s = _dot_nt(ks_ref[g], qa)
        s_scr[g] = s
        return jnp.maximum(m, jnp.max(s, axis=0, keepdims=True))

    def paired(trip, carry):
        def quad(i, c):
            for u in range(4):
                c = trip(4 * i + u, c)
            return c

        carry = lax.fori_loop(0, n_trip // 4, quad, carry)
        rem = n_trip % 4
        base = n_trip - rem
        carry = lax.cond(rem >= 2, lambda c: trip(base + 1, trip(base, c)), lambda c: c, carry)
        return lax.cond(rem % 2 == 1, lambda c: trip(n_trip - 1, c), lambda c: c, carry)

    m_s = paired(score_trip, m_d)

    def value_trip(g, acc):
        p = jnp.exp2(s_scr[g] - m_s).astype(BF16)
        return acc + _dot(vst_ref[g], p)

    acc_s = _dot(vsd_ref[...], jnp.exp2(s_d - m_s).astype(BF16))
    acc_s = paired(value_trip, acc_s)
    o_s = acc_s[0:dh, :] * (1.0 / acc_s[dh:dh + 1, :])

    o_t = o_cw + gates[1] * o_s
    per = LANES // dh
    outs = []
    for hp in range(hpg // per):
        stack = jnp.concatenate([o_t[:, (hp * per + k) * Q:(hp * per + k + 1) * Q]
                                 for k in range(per)], axis=0)
        outs.append(stack.T)
    o_ref[...] = jnp.concatenate(outs, axis=1).astype(o_ref.dtype)


def _nsa(proj3, q_col, gate_col, gate_lane, hpg, cn, ct, ks, vst, ksd, vsd, kwd, vwd, tb, n_sel):
    b, s, _ = proj3.shape
    g, nqb, dh = ks.shape[1], ksd.shape[2], NSA_HEAD_DIM
    nc = cn.shape[3]
    nt, kt, ka = ks.shape[2], ks.shape[3], ks.shape[4]
    ncol = hpg * Q_BLOCK
    qw = hpg * dh
    vr = vst.shape[3]
    once = pl.Buffered(1)
    return pl.pallas_call(
        functools.partial(_nsa_kernel, n_sel=n_sel, nqb=nqb, gate_lane=gate_lane),
        grid=(b, g, nqb),
        in_specs=[
            pl.BlockSpec((None, Q_BLOCK, qw), lambda bi, gi, qi: (bi, qi, q_col // qw + gi)),
            pl.BlockSpec((None, Q_BLOCK, LANES), lambda bi, gi, qi: (bi, qi, gate_col // LANES + gi)),
            pl.BlockSpec((None, None, None, nc, dh), lambda bi, gi, qi: (0, bi, gi, 0, 0)),
            pl.BlockSpec((None, None, None, dh, nc), lambda bi, gi, qi: (1, bi, gi, 0, 0)),
            pl.BlockSpec((None, None, nt, kt, ka), lambda bi, gi, qi: (bi, gi, 0, 0, 0),
                         pipeline_mode=once),
            pl.BlockSpec((None, None, nt, vr, kt), lambda bi, gi, qi: (bi, gi, 0, 0, 0),
                         pipeline_mode=once),
            pl.BlockSpec((None, None, None, Q_BLOCK, dh), lambda bi, gi, qi: (bi, gi, qi, 0, 0)),
            pl.BlockSpec((None, None, None, vr, Q_BLOCK), lambda bi, gi, qi: (bi, gi, qi, 0, 0)),
            pl.BlockSpec((None, None, nqb, Q_BLOCK, dh), lambda bi, gi, qi: (bi, gi, 0, 0, 0),
                         pipeline_mode=once),
            pl.BlockSpec((None, None, nqb, vr, Q_BLOCK), lambda bi, gi, qi: (bi, gi, 0, 0, 0),
                         pipeline_mode=once),
            pl.BlockSpec(tb.shape, lambda bi, gi, qi: (0, 0)),
        ],
        out_specs=pl.BlockSpec((None, Q_BLOCK, qw), lambda bi, gi, qi: (bi, qi, gi)),
        out_shape=jax.ShapeDtypeStruct((b, s, g * qw), BF16),
        scratch_shapes=[pltpu.VMEM((nt, kt, ncol), F32), pltpu.VMEM((nc, Q_BLOCK), F32)],
        compiler_params=_cparams(("parallel", "parallel", "arbitrary")),
        name="nsa",
    )(proj3, proj3, cn, ct, ks, vst, ksd, vsd, kwd, vwd, tb)


def _kvprep_kernel(pc_ref, ps_ref, pw_ref, craw_ref, ks_ref, vst_ref, ksd_ref, vsd_ref,
                   kwd_ref, vwd_ref):
    dh = NSA_HEAD_DIM
    Q = Q_BLOCK
    kt = ps_ref.shape[0]
    nbp = ks_ref.shape[1] - dh
    j = pl.program_id(2)
    for p in range(CMP_STRIDE):
        rows = pc_ref[pl.ds(p, kt // CMP_STRIDE, stride=CMP_STRIDE), :]
        craw_ref[0, :, p * dh:(p + 1) * dh] = rows[:, 0:dh]
        craw_ref[1, :, p * dh:(p + 1) * dh] = rows[:, dh:2 * dh]

    tail_row = lax.broadcasted_iota(jnp.int32, (V_ROWS - dh, kt), 0)
    tail = jnp.where(tail_row == 0, 1.0, 0.0).astype(BF16)

    def k_and_vt(p_ref):
        p = p_ref[...]
        k = p[:, 0:dh].astype(BF16)
        vt = jnp.concatenate([p.T[dh:2 * dh, :].astype(BF16), tail], axis=0)
        return k, vt

    k_s, vt_s = k_and_vt(ps_ref)
    key = lax.broadcasted_iota(jnp.int32, (kt, nbp), 0)
    bcol = lax.broadcasted_iota(jnp.int32, (kt, nbp), 1)
    blk_of_key = j * (kt // SLC_BLOCK) + lax.shift_right_logical(key, SLC_BLOCK.bit_length() - 1)
    ks_ref[:, 0:nbp] = jnp.where(bcol == blk_of_key, 1.0, 0.0).astype(BF16)
    ks_ref[:, nbp:nbp + dh] = k_s
    vst_ref[...] = vt_s
    k_w, vt_w = k_and_vt(pw_ref)
    for c in range(kt // Q):
        ksd_ref[c] = k_s[c * Q:(c + 1) * Q, :]
        vsd_ref[c] = vt_s[:, c * Q:(c + 1) * Q]
        kwd_ref[c] = k_w[c * Q:(c + 1) * Q, :]
        vwd_ref[c] = vt_w[:, c * Q:(c + 1) * Q]


def _kvprep(proj3, kv_col, g, kt, nbp):
    b, s, _ = proj3.shape
    dh = NSA_HEAD_DIM
    nt, nqb, cpt = s // kt, s // Q_BLOCK, kt // Q_BLOCK
    pair = 2 * dh
    base = kv_col // pair

    def pspec(branch):
        return pl.BlockSpec((None, kt, pair), lambda bi, gi, j: (bi, j, base + 3 * gi + branch))

    chunk_k = pl.BlockSpec((None, None, cpt, Q_BLOCK, dh), lambda bi, gi, j: (bi, gi, j, 0, 0))
    chunk_v = pl.BlockSpec((None, None, cpt, V_ROWS, Q_BLOCK), lambda bi, gi, j: (bi, gi, j, 0, 0))
    return pl.pallas_call(
        _kvprep_kernel,
        grid=(b, g, nt),
        in_specs=[pspec(0), pspec(1), pspec(2)],
        out_specs=[
            pl.BlockSpec((2, None, None, kt // CMP_STRIDE, CMP_STRIDE * dh),
                         lambda bi, gi, j: (0, bi, gi, j, 0)),
            pl.BlockSpec((None, None, None, kt, nbp + dh), lambda bi, gi, j: (bi, gi, j, 0, 0)),
            pl.BlockSpec((None, None, None, V_ROWS, kt), lambda bi, gi, j: (bi, gi, j, 0, 0)),
            chunk_k, chunk_v, chunk_k, chunk_v,
        ],
        out_shape=[
            jax.ShapeDtypeStruct((2, b, g, s // CMP_STRIDE, CMP_STRIDE * dh), F32),
            jax.ShapeDtypeStruct((b, g, nt, kt, nbp + dh), BF16),
            jax.ShapeDtypeStruct((b, g, nt, V_ROWS, kt), BF16),
            jax.ShapeDtypeStruct((b, g, nqb, Q_BLOCK, dh), BF16),
            jax.ShapeDtypeStruct((b, g, nqb, V_ROWS, Q_BLOCK), BF16),
            jax.ShapeDtypeStruct((b, g, nqb, Q_BLOCK, dh), BF16),
            jax.ShapeDtypeStruct((b, g, nqb, V_ROWS, Q_BLOCK), BF16),
        ],
        compiler_params=_cparams(("parallel", "parallel", "parallel")),
        name="kvprep",
    )(proj3, proj3, proj3)


def _outproj_kernel(h_ref, ys_ref, yn_ref, yc_ref, w_ref, nw_ref, o_ref):
    w0 = ys_ref.shape[1]
    w1 = w0 + yn_ref.shape[1]
    w2 = w1 + yc_ref.shape[1]
    m = (_dot(ys_ref[...], w_ref[0:w0, :]) + _dot(yn_ref[...], w_ref[w0:w1, :])
         + _dot(yc_ref[...], w_ref[w1:w2, :]))
    o_ref[...] = h_ref[...] + _rms(m, nw_ref[...])


def _outproj(h, ys, yn, yc, w, nw, layer):
    n, d = h.shape
    dm = w.shape[1]
    return pl.pallas_call(
        _outproj_kernel,
        grid=(n // TOKEN_TILE,),
        in_specs=[
            pl.BlockSpec((TOKEN_TILE, d), lambda i: (i, 0)),
            pl.BlockSpec((TOKEN_TILE, ys.shape[1]), lambda i: (i, 0)),
            pl.BlockSpec((TOKEN_TILE, yn.shape[1]), lambda i: (i, 0)),
            pl.BlockSpec((TOKEN_TILE, yc.shape[1]), lambda i: (i, 0)),
            pl.BlockSpec((None, dm, d), lambda i: (layer, 0, 0)),
            pl.BlockSpec((None, 1, d), lambda i: (layer, 0, 0)),
        ],
        out_specs=pl.BlockSpec((TOKEN_TILE, d), lambda i: (i, 0)),
        out_shape=jax.ShapeDtypeStruct((n, d), F32),
        compiler_params=_cparams(("parallel",)),
        name="outproj",
    )(h, ys, yn, yc, w, nw)


def _nsa_constants(nc, nqb):
    upq = Q_BLOCK // CMP_STRIDE
    r = np.arange(nc + upq * (nqb - 1))[:, None]
    tl = np.arange(Q_BLOCK)[None, :]
    tb = np.where(CMP_STRIDE * r + (CMP_BLOCK - 1) - Q_BLOCK * (nqb - 1) <= tl, 0.0, NEG)
    return jnp.asarray(tb, F32)


def kernel(x, ffn1_pre_norm, ffn1_w_gu, ffn1_w_down, ffn1_post_norm, mix_pre_norm, w_in,
           ssd_conv_w, ssd_conv_b, ssd_dt_bias, ssd_a_log, ssd_d, ssd_norm_w,
           nsa_k_pe, nsa_k_w1, nsa_k_b1, nsa_k_w2, nsa_v_pe, nsa_v_w1, nsa_v_b1, nsa_v_w2,
           cnv_dw_w, cnv_dw_b, cnv_ln_w, cnv_ln_b, w_out, mix_post_norm,
           ffn2_pre_norm, ffn2_w_gu, ffn2_w_down, ffn2_post_norm):
    bsz, s, d = x.shape
    depth = w_in.shape[0]
    n = bsz * s
    n_heads = ssd_dt_bias.shape[1]
    ssd_w = n_heads * SSD_HEAD_DIM
    bcw = 2 * SSD_GROUPS * SSD_STATE
    cch = cnv_dw_w.shape[2]
    d_in = w_in.shape[2]
    kvw = NSA_KV_GROUPS * NSA_HEAD_DIM
    n_gate = d_in - (2 * ssd_w + bcw + n_heads + 6 * kvw + 2 * cch)
    nsa_heads = n_gate // (NSA_HEAD_DIM + 3)
    nsa_w = nsa_heads * NSA_HEAD_DIM
    G = NSA_KV_GROUPS
    hpg = nsa_heads // G
    dh = NSA_HEAD_DIM

    o_z, o_xbc, o_dt = 0, ssd_w, ssd_w + ssd_w + bcw
    o_q = o_dt + n_heads
    o_kv = o_q + nsa_w
    o_gl = o_kv + 6 * kvw
    o_glu = o_gl + 3 * nsa_heads
    gl_w = 3 * hpg
    src = [("z", o_z, ssd_w), ("glu", o_glu, 2 * cch), ("xs", o_xbc, ssd_w),
           ("bc", o_xbc + ssd_w, bcw), ("q", o_q, nsa_w)]
    cols, pos, pieces = {}, 0, []
    for name, o, w in src:
        cols[name] = pos
        pieces.append(w_in[:, :, o:o + w].astype(BF16))
        pos += w
    kv_cols = w_in[:, :, o_kv:o_kv + 6 * kvw].astype(BF16).reshape(depth, d, 6, G, dh)
    pieces.append(kv_cols.transpose(0, 1, 3, 2, 4).reshape(depth, d, 6 * kvw))
    cols["kv"] = pos
    pos += 6 * kvw
    zeros = lambda w: jnp.zeros((depth, d, w), BF16)
    cols["dt"] = cols["gate"] = pos
    assert pos % LANES == 0 and n_heads + gl_w <= LANES
    for gi in range(G):
        head = w_in[:, :, o_dt:o_dt + n_heads].astype(BF16) if gi == 0 else zeros(n_heads)
        pieces += [head, w_in[:, :, o_gl + gi * gl_w:o_gl + (gi + 1) * gl_w].astype(BF16),
                   zeros(LANES - n_heads - gl_w)]
        pos += LANES
    col_tile = 10 * LANES
    total = -(-pos // col_tile) * col_tile
    if total > pos:
        pieces.append(zeros(total - pos))
    w_in_r = jnp.concatenate(pieces, axis=2)
    assert cols["z"] % ssd_w == 0 and cols["xs"] % ssd_w == 0 and cols["bc"] % bcw == 0
    assert cols["glu"] % (2 * cch) == 0 and cols["q"] % (hpg * dh) == 0 and cols["kv"] % (2 * dh) == 0

    r3 = lambda a: a.reshape(depth, 1, -1)
    w1_gu, w1_dn = ffn1_w_gu.astype(BF16), ffn1_w_down.astype(BF16)
    w2_gu, w2_dn = ffn2_w_gu.astype(BF16), ffn2_w_down.astype(BF16)
    w_out16 = w_out.astype(BF16)
    cwx, cwb = ssd_conv_w[:, :, :ssd_w], ssd_conv_w[:, :, ssd_w:]
    cbx, cbb = r3(ssd_conv_b[:, :ssd_w]), r3(ssd_conv_b[:, ssd_w:])
    padl = lambda a: jnp.pad(a, ((0, 0), (0, LANES - a.shape[1]))).reshape(depth, 1, LANES)
    dtb, alog = padl(ssd_dt_bias), padl(ssd_a_log)
    dsk = r3(jnp.repeat(ssd_d, SSD_HEAD_DIM, axis=1))
    e_np = np.zeros((LANES, ssd_w), np.float32)
    for hh in range(n_heads):
        e_np[hh, hh * SSD_HEAD_DIM:(hh + 1) * SSD_HEAD_DIM] = 1.0
    e_mat = jnp.asarray(e_np, dtype=BF16)
    pe = jnp.stack([nsa_k_pe, nsa_v_pe]).reshape(2, depth, 1, CMP_BLOCK * dh)
    cw1 = jnp.stack([nsa_k_w1, nsa_v_w1]).astype(BF16)
    cb1 = jnp.stack([nsa_k_b1, nsa_v_b1]).reshape(2, depth, 1, -1)
    cw2 = jnp.stack([nsa_k_w2, nsa_v_w2]).astype(BF16)
    cw2t = jnp.swapaxes(cw2, 2, 3)
    nc = s // CMP_STRIDE
    nb = s // SLC_BLOCK
    nbp = -(-nb // LANES) * LANES
    nqb = s // Q_BLOCK
    kt = min(KV_TILE, s)
    nt = s // kt
    n_sel = min(SLC_TOPN, nb)
    tb = _nsa_constants(nc, nqb)
    assert CMP_BLOCK == 2 * CMP_STRIDE and SLC_BLOCK % CMP_STRIDE == 0

    def nsa_branch(proj3, l):
        u16, ks, vst, ksd, vsd, kwd, vwd = _kvprep(proj3, cols["kv"], G, kt, nbp)
        cn, ct = _cmp(u16, pe, cw1, cb1, cw2, cw2t, l)
        return _nsa(proj3, cols["q"], cols["gate"], n_heads, hpg, cn, ct, ks, vst, ksd, vsd,
                    kwd, vwd, tb, n_sel)

    h = x.reshape(n, d)
    for l in range(depth):
        h = _ffn(h, r3(ffn1_pre_norm), w1_gu, w1_dn, r3(ffn1_post_norm), l)

        proj = _inproj(h, r3(mix_pre_norm), w_in_r, l, col_tile)
        proj3 = proj.reshape(bsz, s, total)
        y_ssd = _ssd(proj3, cols, cwx, cbx, cwb, cbb, dtb, alog, dsk, r3(ssd_norm_w), e_mat, l, n_heads)
        y_cnv = _cnv(proj3, cols["glu"], cnv_dw_w, r3(cnv_dw_b), r3(cnv_ln_w), r3(cnv_ln_b), l)
        y_nsa = nsa_branch(proj3, l)

        h = _outproj(h, y_ssd.reshape(n, ssd_w), y_nsa.reshape(n, nsa_w), y_cnv.reshape(n, cch),
                     w_out16, r3(mix_post_norm), l)
        h = _ffn(h, r3(ffn2_pre_norm), w2_gu, w2_dn, r3(ffn2_post_norm), l)
    return h.reshape(bsz, s, d)
```

```python
import functools
import math

import numpy as np
import jax
import jax.numpy as jnp
from jax import lax
from jax.experimental import pallas as pl
from jax.experimental.pallas import tpu as pltpu

F32 = jnp.float32
BF16 = jnp.bfloat16

SSD_HEAD_DIM = 64
SSD_GROUPS = 2
SSD_STATE = 128
SSD_CONV = 4
SSD_CHUNK = 128
NSA_HEAD_DIM = 64
NSA_KV_GROUPS = 2
CMP_BLOCK = 32
CMP_STRIDE = 16
SLC_BLOCK = 64
SLC_TOPN = 16
WINDOW = 512
Q_BLOCK = 128
FORCE_BONUS = 1000.0
CONV_KERNEL = 31
HALF = 0.5
EPS = 1e-6
NEG = -1e30

LANES = 128
SUBLANES = 8
BF16_ROWS = 16
VMEM_LIMIT = 56 * 1024 * 1024
TOKEN_TILE = 512
PROJ_TOKEN_TILE = 1024
FF_TILE = 512
KV_TILE = 1024
CONV_TILE = 512
CONV_HALO = 32
V_ROWS = NSA_HEAD_DIM + BF16_ROWS


def _cparams(sem):
    return pltpu.CompilerParams(dimension_semantics=sem, vmem_limit_bytes=VMEM_LIMIT)


def _rms(x, w):
    return (x * lax.rsqrt(jnp.mean(x * x, axis=-1, keepdims=True) + EPS)) * w


def _silu(x):
    return x * jax.nn.sigmoid(x)


def _dot(a, b):
    return jnp.dot(a, b, preferred_element_type=F32)


def _dot_nt(a, b):
    return lax.dot_general(a, b, (((1,), (1,)), ((), ())), preferred_element_type=F32)


def _split3(x):
    hi = x.astype(BF16)
    r1 = x - hi.astype(F32)
    mid = r1.astype(BF16)
    lo = (r1 - mid.astype(F32)).astype(BF16)
    return hi, mid, lo


def _dot_exact_lhs(a, x):
    hi, mid, lo = _split3(x)
    return _dot(a, hi) + _dot(a, mid) + _dot(a, lo)


def _ffn_kernel(h_ref, pre_ref, wg_ref, wu_ref, wd_ref, post_ref, o_ref, u_scr, acc_scr, *, nf):
    j = pl.program_id(1)
    last = nf - 1

    def step(first, final):
        if first:
            u = _rms(h_ref[...], pre_ref[...]).astype(BF16)
            u_scr[...] = u
        else:
            u = u_scr[...]
        g = _dot(u, wg_ref[...])
        v = _dot(u, wu_ref[...])
        part = _dot((_silu(g) * v).astype(BF16), wd_ref[...])
        acc = part if first else acc_scr[...] + part
        if final:
            o_ref[...] = h_ref[...] + HALF * _rms(acc, post_ref[...])
        else:
            acc_scr[...] = acc

    if nf == 1:
        step(True, True)
    else:
        pl.when(j == 0)(lambda: step(True, False))
        pl.when((j > 0) & (j < last))(lambda: step(False, False))
        pl.when(j == last)(lambda: step(False, True))


def _ffn(h, pre_w, w_gu, w_down, post_w, layer):
    n, d = h.shape
    f = w_down.shape[1]
    nf = f // FF_TILE
    return pl.pallas_call(
        functools.partial(_ffn_kernel, nf=nf),
        grid=(n // TOKEN_TILE, nf),
        in_specs=[
            pl.BlockSpec((TOKEN_TILE, d), lambda i, j: (i, 0)),
            pl.BlockSpec((None, 1, d), lambda i, j: (layer, 0, 0)),
            pl.BlockSpec((None, d, FF_TILE), lambda i, j: (layer, 0, j)),
            pl.BlockSpec((None, d, FF_TILE), lambda i, j: (layer, 0, j + nf)),
            pl.BlockSpec((None, FF_TILE, d), lambda i, j: (layer, j, 0)),
            pl.BlockSpec((None, 1, d), lambda i, j: (layer, 0, 0)),
        ],
        out_specs=pl.BlockSpec((TOKEN_TILE, d), lambda i, j: (i, 0)),
        out_shape=jax.ShapeDtypeStruct((n, d), F32),
        scratch_shapes=[pltpu.VMEM((TOKEN_TILE, d), BF16), pltpu.VMEM((TOKEN_TILE, d), F32)],
        compiler_params=_cparams(("parallel", "arbitrary")),
        name="ffn",
    )(h, pre_w, w_gu, w_gu, w_down, post_w)


def _inproj_kernel(h_ref, nw_ref, w_ref, o_ref, u_scr):
    @pl.when(pl.program_id(1) == 0)
    def _():
        u = _rms(h_ref[...], nw_ref[...]).astype(BF16)
        u_scr[...] = u
        o_ref[...] = _dot(u, w_ref[...])

    @pl.when(pl.program_id(1) > 0)
    def _():
        o_ref[...] = _dot(u_scr[...], w_ref[...])


def _inproj(h, nw, w, layer, col_tile):
    n, d = h.shape
    cols = w.shape[2]
    return pl.pallas_call(
        _inproj_kernel,
        grid=(n // PROJ_TOKEN_TILE, cols // col_tile),
        in_specs=[
            pl.BlockSpec((PROJ_TOKEN_TILE, d), lambda i, j: (i, 0)),
            pl.BlockSpec((None, 1, d), lambda i, j: (layer, 0, 0)),
            pl.BlockSpec((None, d, col_tile), lambda i, j: (layer, 0, j)),
        ],
        out_specs=pl.BlockSpec((PROJ_TOKEN_TILE, col_tile), lambda i, j: (i, j)),
        out_shape=jax.ShapeDtypeStruct((n, cols), F32),
        scratch_shapes=[pltpu.VMEM((PROJ_TOKEN_TILE, d), BF16)],
        compiler_params=_cparams(("parallel", "arbitrary")),
        name="inproj",
    )(h, nw, w)


def _ssd_kernel(z_ref, xs_ref, bc_ref, dt_ref, cwx_ref, cbx_ref, cwb_ref, cbb_ref, dtb_ref,
                alog_ref, dsk_ref, nw_ref, e_ref, o_ref, xpad, bpad, hst, *, n_heads):
    L = SSD_CHUNK
    N = SSD_STATE
    P = SSD_HEAD_DIM
    G = SSD_GROUPS
    hg = n_heads // G
    gw = hg * P
    halo = SUBLANES

    @pl.when(pl.program_id(1) == 0)
    def _():
        xpad[0:halo, :] = jnp.zeros((halo, xpad.shape[1]), F32)
        bpad[0:halo, :] = jnp.zeros((halo, bpad.shape[1]), F32)
        hst[...] = jnp.zeros_like(hst)

    xpad[halo:halo + L, :] = xs_ref[...]
    bpad[halo:halo + L, :] = bc_ref[...]

    def conv_silu(pad, w_ref, b_ref):
        full = pad[...]
        acc = b_ref[...]
        for k in range(SSD_CONV):
            back = SSD_CONV - 1 - k
            src = full if back == 0 else pltpu.roll(full, back, 0)
            acc = acc + w_ref[k:k + 1, :] * src[halo:halo + L, :]
        return _silu(acc)

    x = conv_silu(xpad, cwx_ref, cbx_ref)
    bcv = conv_silu(bpad, cwb_ref, cbb_ref)
    xpad[0:halo, :] = xpad[L:L + halo, :]
    bpad[0:halo, :] = bpad[L:L + halo, :]

    lane = lax.broadcasted_iota(jnp.int32, (1, LANES), 1)
    dtv = dt_ref[...] + dtb_ref[...]
    dt = jnp.maximum(dtv, 0.0) + jnp.log(1.0 + jnp.exp(-jnp.abs(dtv)))
    dt = jnp.where(lane < n_heads, dt, 0.0)
    a = -jnp.exp(alog_ref[...])
    da = dt * a
    row = lax.broadcasted_iota(jnp.int32, (L, L), 0)
    col = lax.broadcasted_iota(jnp.int32, (L, L), 1)
    tril = row >= col
    cs = _dot_exact_lhs(tril.astype(BF16), da)
    cs_t = cs.T
    dt_t = dt.T
    cs_last = cs[L - 1:L, :]
    e = e_ref[...]
    ecs = _dot(jnp.exp(cs).astype(BF16), e)
    dend = _dot((jnp.exp(cs_last - cs) * dt).astype(BF16), e)
    lane_p = lax.broadcasted_iota(jnp.int32, (L, 2 * P), 1)

    ys = []
    for g in range(G):
        bm = bcv[:, g * N:(g + 1) * N]
        cm = bcv[:, G * N + g * N:G * N + (g + 1) * N]
        bm16 = bm.astype(BF16)
        cm16 = cm.astype(BF16)
        cb = _dot_nt(cm16, bm16)
        xg = x[:, g * gw:(g + 1) * gw]
        yd = []
        for hp in range(hg // 2):
            ws = []
            for hh in range(2):
                h = g * hg + hp * 2 + hh
                seg = cs[:, h:h + 1] - cs_t[h:h + 1, :]
                dec = jnp.exp(jnp.where(tril, seg, -jnp.inf))
                ws.append(cb * dec * dt_t[h:h + 1, :])
            wcat = jnp.concatenate(ws, axis=1).astype(BF16)
            xp = xg[:, hp * 2 * P:(hp + 1) * 2 * P]
            xbd = jnp.concatenate([jnp.where(lane_p < P, xp, 0.0),
                                   jnp.where(lane_p >= P, xp, 0.0)], axis=0).astype(BF16)
            yd.append(_dot(wcat, xbd))
        y_diag = jnp.concatenate(yd, axis=1)
        hprev = hst[g]
        y_off = _dot(cm16, hprev.astype(BF16)) * ecs[:, g * gw:(g + 1) * gw]
        st = _dot(bm.T.astype(BF16), (xg * dend[:, g * gw:(g + 1) * gw]).astype(BF16))
        hst[g] = hprev * ecs[L - 1:L, g * gw:(g + 1) * gw] + st
        y = y_diag + y_off + dsk_ref[:, g * gw:(g + 1) * gw] * xg
        y = y * _silu(z_ref[:, g * gw:(g + 1) * gw])
        y = y * lax.rsqrt(jnp.mean(y * y, axis=-1, keepdims=True) + EPS)
        ys.append(y * nw_ref[:, g * gw:(g + 1) * gw])
    o_ref[...] = jnp.concatenate(ys, axis=1).astype(o_ref.dtype)


def _ssd(proj3, cols, cwx, cbx, cwb, cbb, dtb, alog, dsk, nw, e, layer, n_heads):
    b, s, _ = proj3.shape
    width = n_heads * SSD_HEAD_DIM
    bcw = 2 * SSD_GROUPS * SSD_STATE
    L = SSD_CHUNK
    gw = width // SSD_GROUPS

    def wspec(c):
        return pl.BlockSpec((None, 1, c), lambda bi, ci: (layer, 0, 0))

    return pl.pallas_call(
        functools.partial(_ssd_kernel, n_heads=n_heads),
        grid=(b, s // L),
        in_specs=[
            pl.BlockSpec((None, L, width), lambda bi, ci: (bi, ci, cols["z"] // width)),
            pl.BlockSpec((None, L, width), lambda bi, ci: (bi, ci, cols["xs"] // width)),
            pl.BlockSpec((None, L, bcw), lambda bi, ci: (bi, ci, cols["bc"] // bcw)),
            pl.BlockSpec((None, L, LANES), lambda bi, ci: (bi, ci, cols["dt"] // LANES)),
            pl.BlockSpec((None, SSD_CONV, width), lambda bi, ci: (layer, 0, 0)),
            wspec(width),
            pl.BlockSpec((None, SSD_CONV, bcw), lambda bi, ci: (layer, 0, 0)),
            wspec(bcw),
            wspec(LANES),
            wspec(LANES),
            wspec(width),
            wspec(width),
            pl.BlockSpec((LANES, width), lambda bi, ci: (0, 0)),
        ],
        out_specs=pl.BlockSpec((None, L, width), lambda bi, ci: (bi, ci, 0)),
        out_shape=jax.ShapeDtypeStruct((b, s, width), BF16),
        scratch_shapes=[pltpu.VMEM((L + SUBLANES, width), F32),
                        pltpu.VMEM((L + SUBLANES, bcw), F32),
                        pltpu.VMEM((SSD_GROUPS, SSD_STATE, gw), F32)],
        compiler_params=_cparams(("parallel", "arbitrary")),
        name="ssd",
    )(proj3, proj3, proj3, proj3, cwx, cbx, cwb, cbb, dtb, alog, dsk, nw, e)


def _cnv_kernel(glu_ref, w_ref, b_ref, lw_ref, lb_ref, o_ref, pad, shifted):
    T = CONV_TILE
    c = pad.shape[1]

    @pl.when(pl.program_id(1) == 0)
    def _():
        pad[0:CONV_HALO, :] = jnp.zeros((CONV_HALO, c), F32)

    pad[CONV_HALO:CONV_HALO + T, :] = glu_ref[:, 0:c] * jax.nn.sigmoid(glu_ref[:, c:2 * c])
    span = T + CONV_HALO - SUBLANES
    for r in range(1, SUBLANES):
        shifted[r - 1] = pad[r:r + span, :]
    acc = b_ref[...]
    for k in range(CONV_KERNEL):
        off = CONV_HALO - (CONV_KERNEL - 1) + k
        r, base = off % SUBLANES, off - off % SUBLANES
        src = pad[base:base + T, :] if r == 0 else shifted[r - 1, base:base + T, :]
        acc = acc + w_ref[k:k + 1, :] * src
    pad[0:CONV_HALO, :] = pad[T:T + CONV_HALO, :]
    mu = jnp.mean(acc, axis=-1, keepdims=True)
    cen = acc - mu
    var = jnp.mean(cen * cen, axis=-1, keepdims=True)
    v = cen * lax.rsqrt(var + EPS) * lw_ref[...] + lb_ref[...]
    o_ref[...] = _silu(v).astype(o_ref.dtype)


def _cnv(proj3, col, w, bias, lw, lb, layer):
    b, s, _ = proj3.shape
    c = w.shape[2]
    T = CONV_TILE
    return pl.pallas_call(
        _cnv_kernel,
        grid=(b, s // T),
        in_specs=[
            pl.BlockSpec((None, T, 2 * c), lambda bi, ti: (bi, ti, col // (2 * c))),
            pl.BlockSpec((None, CONV_KERNEL, c), lambda bi, ti: (layer, 0, 0)),
            pl.BlockSpec((None, 1, c), lambda bi, ti: (layer, 0, 0)),
            pl.BlockSpec((None, 1, c), lambda bi, ti: (layer, 0, 0)),
            pl.BlockSpec((None, 1, c), lambda bi, ti: (layer, 0, 0)),
        ],
        out_specs=pl.BlockSpec((None, T, c), lambda bi, ti: (bi, ti, 0)),
        out_shape=jax.ShapeDtypeStruct((b, s, c), BF16),
        scratch_shapes=[pltpu.VMEM((T + CONV_HALO, c), F32),
                        pltpu.VMEM((SUBLANES - 1, T + CONV_HALO - SUBLANES, c), F32)],
        compiler_params=_cparams(("parallel", "arbitrary")),
        name="cnv",
    )(proj3, w, bias, lw, lb)


def _cmp_kernel(u_ref, pe_ref, w1_ref, b1_ref, w2_ref, w2t_ref, on_ref, ot_ref):
    u = u_ref[...]
    half = u.shape[1]
    nc = u.shape[0]
    a0 = (u + pe_ref[:, 0:half]).astype(BF16)
    a1 = (u + pe_ref[:, half:2 * half]).astype(BF16)
    h0 = _dot(a0, w1_ref[0:half, :])
    h1 = _dot(a1, w1_ref[half:2 * half, :])
    pre = h0 + pltpu.roll(h1, nc - 1, 0) + b1_ref[...]
    act = _silu(pre).astype(BF16)
    on_ref[...] = _dot(act, w2_ref[...]).astype(on_ref.dtype)
    ot_ref[...] = _dot_nt(w2t_ref[...], act).astype(ot_ref.dtype)


def _cmp(u16, pe, w1, b1, w2, w2t, layer):
    _, b, g, nc, uw = u16.shape
    hid = w1.shape[3]
    dh = w2.shape[3]
    return pl.pallas_call(
        _cmp_kernel,
        grid=(2, b, g),
        in_specs=[
            pl.BlockSpec((None, None, None, nc, uw), lambda k, bi, gi: (k, bi, gi, 0, 0)),
            pl.BlockSpec((None, None, 1, 2 * uw), lambda k, bi, gi: (k, layer, 0, 0)),
            pl.BlockSpec((None, None, 2 * uw, hid), lambda k, bi, gi: (k, layer, 0, 0)),
            pl.BlockSpec((None, None, 1, hid), lambda k, bi, gi: (k, layer, 0, 0)),
            pl.BlockSpec((None, None, hid, dh), lambda k, bi, gi: (k, layer, 0, 0)),
            pl.BlockSpec((None, None, dh, hid), lambda k, bi, gi: (k, layer, 0, 0)),
        ],
        out_specs=[
            pl.BlockSpec((None, None, None, nc, dh), lambda k, bi, gi: (k, bi, gi, 0, 0)),
            pl.BlockSpec((None, None, None, dh, nc), lambda k, bi, gi: (k, bi, gi, 0, 0)),
        ],
        out_shape=[jax.ShapeDtypeStruct((2, b, g, nc, dh), BF16),
                   jax.ShapeDtypeStruct((2, b, g, dh, nc), BF16)],
        compiler_params=_cparams(("parallel", "parallel", "parallel")),
        name="cmp",
    )(u16, pe, w1, b1, w2, w2t)


def _nsa_kernel(q_ref, gl_ref, kc_ref, vct_ref, ks_ref, vst_ref, ksd_ref, vsd_ref, kwd_ref,
                vwd_ref, tb_ref, o_ref, s_scr, psum_scr, *, n_sel, nqb, gate_lane):
    Q = Q_BLOCK
    dh = NSA_HEAD_DIM
    hpg = q_ref.shape[1] // dh
    ncol = hpg * Q
    nc = kc_ref.shape[0]
    kt = ks_ref.shape[1]
    nbp = ks_ref.shape[2] - dh
    qi = pl.program_id(2)
    q0 = qi * Q

    qt = q_ref[...] * (dh ** -0.5 * math.log2(math.e))
    q = jnp.concatenate([qt[:, h * dh:(h + 1) * dh] for h in range(hpg)], axis=0).astype(BF16)
    gl_t = gl_ref[...].T
    gates = [jax.nn.sigmoid(jnp.concatenate(
        [gl_t[gate_lane + 3 * h + k:gate_lane + 3 * h + k + 1, :] for h in range(hpg)], axis=1))
        for k in range(3)]
    lane = lax.broadcasted_iota(jnp.int32, (1, ncol), 1)
    tok = q0 + (lane & (Q - 1))

    def tile_cols(a):
        return jnp.concatenate([a] * hpg, axis=1)

    r_i = lax.broadcasted_iota(jnp.int32, (Q, Q), 0)
    c_i = lax.broadcasted_iota(jnp.int32, (Q, Q), 1)
    tri_le = jnp.where(r_i <= c_i, 0.0, NEG)
    tri_gt = jnp.where(r_i > c_i, 0.0, NEG)

    off = pl.multiple_of((nqb - 1 - qi) * (Q // CMP_STRIDE), SUBLANES)
    bias_c = tb_ref[pl.ds(off, nc), :]
    scm = _dot_nt(kc_ref[...], q) + tile_cols(bias_c)
    m_c = jnp.max(scm, axis=0, keepdims=True)
    p_c = jnp.exp2(scm - m_c)
    l_c = jnp.sum(p_c, axis=0, keepdims=True)
    p_c = p_c * jnp.where(tok >= CMP_BLOCK - 1, 1.0 / l_c, 0.0)
    o_c = _dot(vct_ref[...], p_c.astype(BF16))

    psum = p_c[:, 0:Q]
    for h in range(1, hpg):
        psum = psum + p_c[:, h * Q:(h + 1) * Q]
    upb = SLC_BLOCK // CMP_STRIDE
    nb = nc // upb
    psum_scr[...] = psum
    rows = [psum_scr[pl.ds(r, nb, stride=upb), :] for r in range(upb)]
    inner = rows[0]
    for r in range(1, upb - 1):
        inner = inner + rows[r]
    edge = rows[upb - 1]
    first_row = lax.broadcasted_iota(jnp.int32, (nb, Q), 0) == 0
    prev_edge = jnp.where(first_row, 0.0, pltpu.roll(edge, 1, 0))
    imp = (prev_edge + edge) + 2.0 * inner
    if nbp > nb:
        imp = jnp.concatenate([imp, jnp.zeros((nbp - nb, Q), F32)], axis=0)

    nch = WINDOW // Q + 1
    tri_d = tile_cols(tri_le)
    first = qi - (nch - 1)
    idx = [jnp.maximum(first + c, 0) for c in range(nch)]
    before = [jnp.where(first + c >= 0, 0.0, NEG) for c in range(nch)]
    sw = _dot_nt(jnp.concatenate([kwd_ref[i] for i in idx], axis=0), q)
    sw = jnp.concatenate(
        [sw[0:Q] + (tile_cols(tri_gt) + before[0])]
        + [sw[c * Q:(c + 1) * Q] + before[c] for c in range(1, nch - 1)]
        + [sw[(nch - 1) * Q:nch * Q] + tri_d], axis=0)
    m_w = jnp.max(sw, axis=0, keepdims=True)
    p_w = jnp.exp2(sw - m_w).astype(BF16)
    acc_w = jnp.zeros((V_ROWS, ncol), F32)
    for c in range(nch):
        acc_w = acc_w + _dot(vwd_ref[idx[c]], p_w[c * Q:(c + 1) * Q, :])
    o_w = acc_w[0:dh, :] * (1.0 / acc_w[dh:dh + 1, :])
    o_cw = gates[0] * o_c + gates[2] * o_w

    s_d = _dot_nt(ksd_ref[...], q) + tri_d
    m_d = jnp.max(s_d, axis=0, keepdims=True)

    blk = lax.broadcasted_iota(jnp.int32, (nbp, Q), 0)
    t1 = q0 + lax.broadcasted_iota(jnp.int32, (nbp, Q), 1)
    cur = lax.shift_right_logical(t1, SLC_BLOCK.bit_length() - 1)
    forced = (blk == 0) | (blk == cur) | (blk == cur - 1)
    val = jnp.where(blk * SLC_BLOCK <= t1, jnp.where(forced, imp + FORCE_BONUS, imp), NEG)
    blk_f = blk.astype(F32)
    for _ in range(n_sel):
        mx = jnp.max(val, axis=0, keepdims=True)
        idx = jnp.min(jnp.where(val == mx, blk_f, float(nbp)), axis=0, keepdims=True)
        val = jnp.where(blk_f == idx, -jnp.inf, val)
    bias = jnp.where((val == -jnp.inf) & (blk < qi * (Q // SLC_BLOCK)), 0.0, NEG)
    bias_t = tile_cols(bias).T.astype(BF16)
    qa = jnp.concatenate([bias_t, q], axis=1)

    n_trip = (q0 + kt - 1) // kt

    def score_trip(g, m):
        s = _dot_nt(ks_ref[g], qa)
        s_scr[g] = s
        return jnp.maximum(m, jnp.max(s, axis=0, keepdims=True))

    def paired(trip, carry):
        def quad(i, c):
            for u in range(4):
                c = trip(4 * i + u, c)
            return c

        carry = lax.fori_loop(0, n_trip // 4, quad, carry)
        rem = n_trip % 4
        base = n_trip - rem
        carry = lax.cond(rem >= 2, lambda c: trip(base + 1, trip(base, c)), lambda c: c, carry)
        return lax.cond(rem % 2 == 1, lambda c: trip(n_trip - 1, c), lambda c: c, carry)

    m_s = paired(score_trip, m_d)

    def value_trip(g, acc):
        p = jnp.exp2(s_scr[g] - m_s).astype(BF16)
        return acc + _dot(vst_ref[g], p)

    acc_s = _dot(vsd_ref[...], jnp.exp2(s_d - m_s).astype(BF16))
    acc_s = paired(value_trip, acc_s)
    o_s = acc_s[0:dh, :] * (1.0 / acc_s[dh:dh + 1, :])

    o_t = o_cw + gates[1] * o_s
    per = LANES // dh
    outs = []
    for hp in range(hpg // per):
        stack = jnp.concatenate([o_t[:, (hp * per + k) * Q:(hp * per + k + 1) * Q]
                                 for k in range(per)], axis=0)
        outs.append(stack.T)
    o_ref[...] = jnp.concatenate(outs, axis=1).astype(o_ref.dtype)


def _nsa(proj3, q_col, gate_col, gate_lane, hpg, cn, ct, ks, vst, ksd, vsd, kwd, vwd, tb, n_sel):
    b, s, _ = proj3.shape
    g, nqb, dh = ks.shape[1], ksd.shape[2], NSA_HEAD_DIM
    nc = cn.shape[3]
    nt, kt, ka = ks.shape[2], ks.shape[3], ks.shape[4]
    ncol = hpg * Q_BLOCK
    qw = hpg * dh
    vr = vst.shape[3]
    once = pl.Buffered(1)
    return pl.pallas_call(
        functools.partial(_nsa_kernel, n_sel=n_sel, nqb=nqb, gate_lane=gate_lane),
        grid=(b, g, nqb),
        in_specs=[
            pl.BlockSpec((None, Q_BLOCK, qw), lambda bi, gi, qi: (bi, qi, q_col // qw + gi)),
            pl.BlockSpec((None, Q_BLOCK, LANES), lambda bi, gi, qi: (bi, qi, gate_col // LANES + gi)),
            pl.BlockSpec((None, None, None, nc, dh), lambda bi, gi, qi: (0, bi, gi, 0, 0)),
            pl.BlockSpec((None, None, None, dh, nc), lambda bi, gi, qi: (1, bi, gi, 0, 0)),
            pl.BlockSpec((None, None, nt, kt, ka), lambda bi, gi, qi: (bi, gi, 0, 0, 0),
                         pipeline_mode=once),
            pl.BlockSpec((None, None, nt, vr, kt), lambda bi, gi, qi: (bi, gi, 0, 0, 0),
                         pipeline_mode=once),
            pl.BlockSpec((None, None, None, Q_BLOCK, dh), lambda bi, gi, qi: (bi, gi, qi, 0, 0)),
            pl.BlockSpec((None, None, None, vr, Q_BLOCK), lambda bi, gi, qi: (bi, gi, qi, 0, 0)),
            pl.BlockSpec((None, None, nqb, Q_BLOCK, dh), lambda bi, gi, qi: (bi, gi, 0, 0, 0),
                         pipeline_mode=once),
            pl.BlockSpec((None, None, nqb, vr, Q_BLOCK), lambda bi, gi, qi: (bi, gi, 0, 0, 0),
                         pipeline_mode=once),
            pl.BlockSpec(tb.shape, lambda bi, gi, qi: (0, 0)),
        ],
        out_specs=pl.BlockSpec((None, Q_BLOCK, qw), lambda bi, gi, qi: (bi, qi, gi)),
        out_shape=jax.ShapeDtypeStruct((b, s, g * qw), BF16),
        scratch_shapes=[pltpu.VMEM((nt, kt, ncol), F32), pltpu.VMEM((nc, Q_BLOCK), F32)],
        compiler_params=_cparams(("parallel", "parallel", "arbitrary")),
        name="nsa",
    )(proj3, proj3, cn, ct, ks, vst, ksd, vsd, kwd, vwd, tb)


def _kvprep_kernel(pc_ref, ps_ref, pw_ref, craw_ref, ks_ref, vst_ref, ksd_ref, vsd_ref,
                   kwd_ref, vwd_ref):
    dh = NSA_HEAD_DIM
    Q = Q_BLOCK
    kt = ps_ref.shape[0]
    nbp = ks_ref.shape[1] - dh
    j = pl.program_id(2)
    for p in range(CMP_STRIDE):
        rows = pc_ref[pl.ds(p, kt // CMP_STRIDE, stride=CMP_STRIDE), :]
        craw_ref[0, :, p * dh:(p + 1) * dh] = rows[:, 0:dh]
        craw_ref[1, :, p * dh:(p + 1) * dh] = rows[:, dh:2 * dh]

    tail_row = lax.broadcasted_iota(jnp.int32, (V_ROWS - dh, kt), 0)
    tail = jnp.where(tail_row == 0, 1.0, 0.0).astype(BF16)

    def k_and_vt(p_ref):
        p = p_ref[...]
        k = p[:, 0:dh].astype(BF16)
        vt = jnp.concatenate([p.T[dh:2 * dh, :].astype(BF16), tail], axis=0)
        return k, vt

    k_s, vt_s = k_and_vt(ps_ref)
    key = lax.broadcasted_iota(jnp.int32, (kt, nbp), 0)
    bcol = lax.broadcasted_iota(jnp.int32, (kt, nbp), 1)
    blk_of_key = j * (kt // SLC_BLOCK) + lax.shift_right_logical(key, SLC_BLOCK.bit_length() - 1)
    ks_ref[:, 0:nbp] = jnp.where(bcol == blk_of_key, 1.0, 0.0).astype(BF16)
    ks_ref[:, nbp:nbp + dh] = k_s
    vst_ref[...] = vt_s
    k_w, vt_w = k_and_vt(pw_ref)
    for c in range(kt // Q):
        ksd_ref[c] = k_s[c * Q:(c + 1) * Q, :]
        vsd_ref[c] = vt_s[:, c * Q:(c + 1) * Q]
        kwd_ref[c] = k_w[c * Q:(c + 1) * Q, :]
        vwd_ref[c] = vt_w[:, c * Q:(c + 1) * Q]


def _kvprep(proj3, kv_col, g, kt, nbp):
    b, s, _ = proj3.shape
    dh = NSA_HEAD_DIM
    nt, nqb, cpt = s // kt, s // Q_BLOCK, kt // Q_BLOCK
    pair = 2 * dh
    base = kv_col // pair

    def pspec(branch):
        return pl.BlockSpec((None, kt, pair), lambda bi, gi, j: (bi, j, base + 3 * gi + branch))

    chunk_k = pl.BlockSpec((None, None, cpt, Q_BLOCK, dh), lambda bi, gi, j: (bi, gi, j, 0, 0))
    chunk_v = pl.BlockSpec((None, None, cpt, V_ROWS, Q_BLOCK), lambda bi, gi, j: (bi, gi, j, 0, 0))
    return pl.pallas_call(
        _kvprep_kernel,
        grid=(b, g, nt),
        in_specs=[pspec(0), pspec(1), pspec(2)],
        out_specs=[
            pl.BlockSpec((2, None, None, kt // CMP_STRIDE, CMP_STRIDE * dh),
                         lambda bi, gi, j: (0, bi, gi, j, 0)),
            pl.BlockSpec((None, None, None, kt, nbp + dh), lambda bi, gi, j: (bi, gi, j, 0, 0)),
            pl.BlockSpec((None, None, None, V_ROWS, kt), lambda bi, gi, j: (bi, gi, j, 0, 0)),
            chunk_k, chunk_v, chunk_k, chunk_v,
        ],
        out_shape=[
            jax.ShapeDtypeStruct((2, b, g, s // CMP_STRIDE, CMP_STRIDE * dh), F32),
            jax.ShapeDtypeStruct((b, g, nt, kt, nbp + dh), BF16),
            jax.ShapeDtypeStruct((b, g, nt, V_ROWS, kt), BF16),
            jax.ShapeDtypeStruct((b, g, nqb, Q_BLOCK, dh), BF16),
            jax.ShapeDtypeStruct((b, g, nqb, V_ROWS, Q_BLOCK), BF16),
            jax.ShapeDtypeStruct((b, g, nqb, Q_BLOCK, dh), BF16),
            jax.ShapeDtypeStruct((b, g, nqb, V_ROWS, Q_BLOCK), BF16),
        ],
        compiler_params=_cparams(("parallel", "parallel", "parallel")),
        name="kvprep",
    )(proj3, proj3, proj3)


def _outproj_kernel(h_ref, ys_ref, yn_ref, yc_ref, w_ref, nw_ref, o_ref):
    w0 = ys_ref.shape[1]
    w1 = w0 + yn_ref.shape[1]
    w2 = w1 + yc_ref.shape[1]
    m = (_dot(ys_ref[...], w_ref[0:w0, :]) + _dot(yn_ref[...], w_ref[w0:w1, :])
         + _dot(yc_ref[...], w_ref[w1:w2, :]))
    o_ref[...] = h_ref[...] + _rms(m, nw_ref[...])


def _outproj(h, ys, yn, yc, w, nw, layer):
    n, d = h.shape
    dm = w.shape[1]
    return pl.pallas_call(
        _outproj_kernel,
        grid=(n // TOKEN_TILE,),
        in_specs=[
            pl.BlockSpec((TOKEN_TILE, d), lambda i: (i, 0)),
            pl.BlockSpec((TOKEN_TILE, ys.shape[1]), lambda i: (i, 0)),
            pl.BlockSpec((TOKEN_TILE, yn.shape[1]), lambda i: (i, 0)),
            pl.BlockSpec((TOKEN_TILE, yc.shape[1]), lambda i: (i, 0)),
            pl.BlockSpec((None, dm, d), lambda i: (layer, 0, 0)),
            pl.BlockSpec((None, 1, d), lambda i: (layer, 0, 0)),
        ],
        out_specs=pl.BlockSpec((TOKEN_TILE, d), lambda i: (i, 0)),
        out_shape=jax.ShapeDtypeStruct((n, d), F32),
        compiler_params=_cparams(("parallel",)),
        name="outproj",
    )(h, ys, yn, yc, w, nw)


def _nsa_constants(nc, nqb):
    upq = Q_BLOCK // CMP_STRIDE
    r = np.arange(nc + upq * (nqb - 1))[:, None]
    tl = np.arange(Q_BLOCK)[None, :]
    tb = np.where(CMP_STRIDE * r + (CMP_BLOCK - 1) - Q_BLOCK * (nqb - 1) <= tl, 0.0, NEG)
    return jnp.asarray(tb, F32)


def kernel(x, ffn1_pre_norm, ffn1_w_gu, ffn1_w_down, ffn1_post_norm, mix_pre_norm, w_in,
           ssd_conv_w, ssd_conv_b, ssd_dt_bias, ssd_a_log, ssd_d, ssd_norm_w,
           nsa_k_pe, nsa_k_w1, nsa_k_b1, nsa_k_w2, nsa_v_pe, nsa_v_w1, nsa_v_b1, nsa_v_w2,
           cnv_dw_w, cnv_dw_b, cnv_ln_w, cnv_ln_b, w_out, mix_post_norm,
           ffn2_pre_norm, ffn2_w_gu, ffn2_w_down, ffn2_post_norm):
    bsz, s, d = x.shape
    depth = w_in.shape[0]
    n = bsz * s
    n_heads = ssd_dt_bias.shape[1]
    ssd_w = n_heads * SSD_HEAD_DIM
    bcw = 2 * SSD_GROUPS * SSD_STATE
    cch = cnv_dw_w.shape[2]
    d_in = w_in.shape[2]
    kvw = NSA_KV_GROUPS * NSA_HEAD_DIM
    n_gate = d_in - (2 * ssd_w + bcw + n_heads + 6 * kvw + 2 * cch)
    nsa_heads = n_gate // (NSA_HEAD_DIM + 3)
    nsa_w = nsa_heads * NSA_HEAD_DIM
    G = NSA_KV_GROUPS
    hpg = nsa_heads // G
    dh = NSA_HEAD_DIM

    o_z, o_xbc, o_dt = 0, ssd_w, ssd_w + ssd_w + bcw
    o_q = o_dt + n_heads
    o_kv = o_q + nsa_w
    o_gl = o_kv + 6 * kvw
    o_glu = o_gl + 3 * nsa_heads
    gl_w = 3 * hpg
    src = [("z", o_z, ssd_w), ("glu", o_glu, 2 * cch), ("xs", o_xbc, ssd_w),
           ("bc", o_xbc + ssd_w, bcw), ("q", o_q, nsa_w)]
    cols, pos, pieces = {}, 0, []
    for name, o, w in src:
        cols[name] = pos
        pieces.append(w_in[:, :, o:o + w].astype(BF16))
        pos += w
    kv_cols = w_in[:, :, o_kv:o_kv + 6 * kvw].astype(BF16).reshape(depth, d, 6, G, dh)
    pieces.append(kv_cols.transpose(0, 1, 3, 2, 4).reshape(depth, d, 6 * kvw))
    cols["kv"] = pos
    pos += 6 * kvw
    zeros = lambda w: jnp.zeros((depth, d, w), BF16)
    cols["dt"] = cols["gate"] = pos
    assert pos % LANES == 0 and n_heads + gl_w <= LANES
    for gi in range(G):
        head = w_in[:, :, o_dt:o_dt + n_heads].astype(BF16) if gi == 0 else zeros(n_heads)
        pieces += [head, w_in[:, :, o_gl + gi * gl_w:o_gl + (gi + 1) * gl_w].astype(BF16),
                   zeros(LANES - n_heads - gl_w)]
        pos += LANES
    col_tile = 10 * LANES
    total = -(-pos // col_tile) * col_tile
    if total > pos:
        pieces.append(zeros(total - pos))
    w_in_r = jnp.concatenate(pieces, axis=2)
    assert cols["z"] % ssd_w == 0 and cols["xs"] % ssd_w == 0 and cols["bc"] % bcw == 0
    assert cols["glu"] % (2 * cch) == 0 and cols["q"] % (hpg * dh) == 0 and cols["kv"] % (2 * dh) == 0

    r3 = lambda a: a.reshape(depth, 1, -1)
    w1_gu, w1_dn = ffn1_w_gu.astype(BF16), ffn1_w_down.astype(BF16)
    w2_gu, w2_dn = ffn2_w_gu.astype(BF16), ffn2_w_down.astype(BF16)
    w_out16 = w_out.astype(BF16)
    cwx, cwb = ssd_conv_w[:, :, :ssd_w], ssd_conv_w[:, :, ssd_w:]
    cbx, cbb = r3(ssd_conv_b[:, :ssd_w]), r3(ssd_conv_b[:, ssd_w:])
    padl = lambda a: jnp.pad(a, ((0, 0), (0, LANES - a.shape[1]))).reshape(depth, 1, LANES)
    dtb, alog = padl(ssd_dt_bias), padl(ssd_a_log)
    dsk = r3(jnp.repeat(ssd_d, SSD_HEAD_DIM, axis=1))
    e_np = np.zeros((LANES, ssd_w), np.float32)
    for hh in range(n_heads):
        e_np[hh, hh * SSD_HEAD_DIM:(hh + 1) * SSD_HEAD_DIM] = 1.0
    e_mat = jnp.asarray(e_np, dtype=BF16)
    pe = jnp.stack([nsa_k_pe, nsa_v_pe]).reshape(2, depth, 1, CMP_BLOCK * dh)
    cw1 = jnp.stack([nsa_k_w1, nsa_v_w1]).astype(BF16)
    cb1 = jnp.stack([nsa_k_b1, nsa_v_b1]).reshape(2, depth, 1, -1)
    cw2 = jnp.stack([nsa_k_w2, nsa_v_w2]).astype(BF16)
    cw2t = jnp.swapaxes(cw2, 2, 3)
    nc = s // CMP_STRIDE
    nb = s // SLC_BLOCK
    nbp = -(-nb // LANES) * LANES
    nqb = s // Q_BLOCK
    kt = min(KV_TILE, s)
    nt = s // kt
    n_sel = min(SLC_TOPN, nb)
    tb = _nsa_constants(nc, nqb)
    assert CMP_BLOCK == 2 * CMP_STRIDE and SLC_BLOCK % CMP_STRIDE == 0

    def nsa_branch(proj3, l):
        u16, ks, vst, ksd, vsd, kwd, vwd = _kvprep(proj3, cols["kv"], G, kt, nbp)
        cn, ct = _cmp(u16, pe, cw1, cb1, cw2, cw2t, l)
        return _nsa(proj3, cols["q"], cols["gate"], n_heads, hpg, cn, ct, ks, vst, ksd, vsd,
                    kwd, vwd, tb, n_sel)

    h = x.reshape(n, d)
    for l in range(depth):
        h = _ffn(h, r3(ffn1_pre_norm), w1_gu, w1_dn, r3(ffn1_post_norm), l)

        proj = _inproj(h, r3(mix_pre_norm), w_in_r, l, col_tile)
        proj3 = proj.reshape(bsz, s, total)
        y_ssd = _ssd(proj3, cols, cwx, cbx, cwb, cbb, dtb, alog, dsk, r3(ssd_norm_w), e_mat, l, n_heads)
        y_cnv = _cnv(proj3, cols["glu"], cnv_dw_w, r3(cnv_dw_b), r3(cnv_ln_w), r3(cnv_ln_b), l)
        y_nsa = nsa_branch(proj3, l)

        h = _outproj(h, y_ssd.reshape(n, ssd_w), y_nsa.reshape(n, nsa_w), y_cnv.reshape(n, cch),
                     w_out16, r3(mix_post_norm), l)
        h = _ffn(h, r3(ffn2_pre_norm), w2_gu, w2_dn, r3(ffn2_post_norm), l)
    return h.reshape(bsz, s, d)
```
